```python
import math
import jax, jax.numpy as jnp
from jax import lax
import numpy as np

D_MODEL = 4096
BATCH = 2
SEQ = 4096
DEPTH = 4

GRID_W = 64
CTX_LEN = 256
N_MIXERS = 3
COND_DIM = 1024
N_MOD = 6
RMS_EPS = 1e-6
ROPE_THETA = 10000.0
Q_BLOCK = 128
DA_HEADS = 16
DA_HEAD_DIM = 128
NA_HEADS = 32
NA_HEAD_DIM = 128
NA_KH = 8
NA_KW = 16
SW_HEADS = 64
SW_KV_HEADS = 8
SW_HEAD_DIM = 64
SW_WINDOW = 128
N_EXPERTS = 16
N_GROUPS = 4
EXPERTS_PER_GROUP = N_EXPERTS // N_GROUPS
TOP_K = 2
D_FF = 768
N_A = len(range(0, DEPTH, N_MIXERS))
N_B = len(range(1, DEPTH, N_MIXERS))
N_C = len(range(2, DEPTH, N_MIXERS))

kernel_name = "hybrid_diffattn_natten_swa_groupmoe_dit"


def rms_norm(x, g):
    x32 = x.astype(jnp.float32)
    y = x32 * lax.rsqrt(jnp.mean(x32 * x32, axis=-1, keepdims=True) + RMS_EPS)
    return (y * g.astype(jnp.float32)).astype(x.dtype)


def grid_positions(t):
    pos = jnp.arange(t)
    return (pos // GRID_W).astype(jnp.float32), (pos % GRID_W).astype(jnp.float32)


def axial_rope(x, row, col):
    d = x.shape[-1]
    half, quarter = d // 2, d // 4
    inv = ROPE_THETA ** (-jnp.arange(quarter, dtype=jnp.float32) / quarter)

    def rot(xa, pos):
        ang = pos[:, None] * inv[None, :]
        cos, sin = jnp.cos(ang)[None, :, None, :], jnp.sin(ang)[None, :, None, :]
        x1, x2 = xa[..., :quarter], xa[..., quarter:]
        return jnp.concatenate([x1 * cos - x2 * sin, x1 * sin + x2 * cos], axis=-1)

    x32 = x.astype(jnp.float32)
    out = jnp.concatenate([rot(x32[..., :half], row), rot(x32[..., half:], col)], axis=-1)
    return out.astype(x.dtype)


def diff_attention_mixer(hx, hc, w_qkv, w_o, lam_p, subln_g, lam_init, row, col, ctx_out):
    B, S, _ = hx.shape
    H, d = DA_HEADS, DA_HEAD_DIM
    qd = H * 2 * d

    def project(h):
        qkv = h @ w_qkv
        q = qkv[..., :qd].reshape(B, -1, H, 2, d)
        k = qkv[..., qd:2 * qd].reshape(B, -1, H, 2, d)
        v = qkv[..., 2 * qd:].reshape(B, -1, H, 2 * d)
        return q, k, v

    qx, kx, vx = project(hx)
    qc, kc, vc = project(hc)
    qx = axial_rope(qx.reshape(B, S, H * 2, d), row, col).reshape(B, S, H, 2, d)
    kx = axial_rope(kx.reshape(B, S, H * 2, d), row, col).reshape(B, S, H, 2, d)
    lp = lam_p.astype(jnp.float32)
    lam = jnp.exp(jnp.sum(lp[0] * lp[1])) - jnp.exp(jnp.sum(lp[2] * lp[3])) + lam_init
    scale = d ** -0.5

    def attend(q, k, v):
        s = jnp.einsum('bqhmd,bkhmd->bhmqk', q, k).astype(jnp.float32) * scale
        p = jax.nn.softmax(s, axis=-1)
        w = (p[:, :, 0] - lam * p[:, :, 1]).astype(v.dtype)
        return jnp.einsum('bhqk,bkhe->bqhe', w, v)

    k_all = jnp.concatenate([kc, kx], axis=1)
    v_all = jnp.concatenate([vc, vx], axis=1)
    nb = S // Q_BLOCK
    qb = qx.reshape(B, nb, Q_BLOCK, H, 2, d).swapaxes(0, 1)
    ox = lax.map(lambda q: attend(q, k_all, v_all), qb)
    ox = ox.swapaxes(0, 1).reshape(B, S, H, 2 * d)

    def finish(o):
        o = rms_norm(o, subln_g) * (1.0 - lam_init)
        return o.reshape(o.shape[0], o.shape[1], H * 2 * d) @ w_o

    yx = finish(ox)
    yc = finish(attend(qc, kc, vc)) if ctx_out else None
    return yx, yc


def neighbourhood_mixer(hx, hc, w_qkv, w_o, rpb, ctx_out):
    B, S, _ = hx.shape
    C = hc.shape[1]
    H, d = NA_HEADS, NA_HEAD_DIM
    rows = S // GRID_W
    kh, kw = min(NA_KH, rows), NA_KW
    qkv_x = (hx @ w_qkv).reshape(B, S, 3, H, d)
    qkv_c = (hc @ w_qkv).reshape(B, C, 3, H, d)
    qx, kx, vx = qkv_x[:, :, 0], qkv_x[:, :, 1], qkv_x[:, :, 2]
    qc, kc, vc = qkv_c[:, :, 0], qkv_c[:, :, 1], qkv_c[:, :, 2]
    scale = d ** -0.5
    kg = kx.reshape(B, rows, GRID_W, H, d)
    vg = vx.reshape(B, rows, GRID_W, H, d)
    cols = np.arange(GRID_W)
    col_idx = np.clip(cols - kw // 2, 0, GRID_W - kw)[:, None] + np.arange(kw)[None, :]
    col_bias_idx = col_idx - cols[:, None] + (NA_KW - 1)
    rpb_cols = rpb[:, :, col_bias_idx]

    def row_block(args):
        r, q = args
        rs = jnp.clip(r - kh // 2, 0, rows - kh)
        kr = lax.dynamic_slice_in_dim(kg, rs, kh, axis=1)[:, :, col_idx]
        vr = lax.dynamic_slice_in_dim(vg, rs, kh, axis=1)[:, :, col_idx]
        s_loc = jnp.einsum('bwhd,bawjhd->bhwaj', q, kr).astype(jnp.float32) * scale
        row_bias_idx = rs + jnp.arange(kh) - r + (NA_KH - 1)
        bias = jnp.take(rpb_cols, row_bias_idx, axis=1).transpose(0, 2, 1, 3)
        s_loc = s_loc + bias[None].astype(jnp.float32)
        s_ctx = jnp.einsum('bwhd,bchd->bhwc', q, kc).astype(jnp.float32) * scale
        s = jnp.concatenate([s_ctx, s_loc.reshape(B, H, GRID_W, kh * kw)], axis=-1)
        p = jax.nn.softmax(s, axis=-1).astype(vx.dtype)
        p_ctx, p_loc = p[..., :C], p[..., C:].reshape(B, H, GRID_W, kh, kw)
        return (jnp.einsum('bhwc,bchd->bwhd', p_ctx, vc)
                + jnp.einsum('bhwaj,bawjhd->bwhd', p_loc, vr))

    qg = qx.reshape(B, rows, GRID_W, H, d).swapaxes(0, 1)
    ox = lax.map(row_block, (jnp.arange(rows), qg))
    yx = ox.swapaxes(0, 1).reshape(B, S, H * d) @ w_o
    yc = None
    if ctx_out:
        s = jnp.einsum('bqhd,bkhd->bhqk', qc, kc).astype(jnp.float32) * scale
        p = jax.nn.softmax(s, axis=-1).astype(vc.dtype)
        yc = jnp.einsum('bhqk,bkhd->bqhd', p, vc).reshape(B, C, H * d) @ w_o
    return yx, yc


def window_gqa_mixer(hx, hc, w_qkv, w_o, sink, row, col, ctx_out):
    B, S, _ = hx.shape
    C = hc.shape[1]
    H, KV, d = SW_HEADS, SW_KV_HEADS, SW_HEAD_DIM
    G = H // KV

    def project(h):
        qkv = h @ w_qkv
        q = qkv[..., :H * d].reshape(B, -1, H, d)
        k = qkv[..., H * d:(H + KV) * d].reshape(B, -1, KV, d)
        v = qkv[..., (H + KV) * d:].reshape(B, -1, KV, d)
        return q, k, v

    qx, kx, vx = project(hx)
    qc, kc, vc = project(hc)
    qx = axial_rope(qx, row, col).reshape(B, S, KV, G, d)
    kx = axial_rope(kx, row, col)
    qc = qc.reshape(B, C, KV, G, d)
    sink_l = sink.reshape(KV, G).astype(jnp.float32)
    scale = d ** -0.5

    def softmax_with_sink(s):
        sk = jnp.broadcast_to(sink_l[None, :, :, None, None], s.shape[:-1] + (1,))
        return jax.nn.softmax(jnp.concatenate([s, sk], axis=-1), axis=-1)[..., :-1]

    pad = SW_WINDOW
    band = Q_BLOCK + 2 * pad
    kp = jnp.pad(kx, ((0, 0), (pad, pad), (0, 0), (0, 0)))
    vp = jnp.pad(vx, ((0, 0), (pad, pad), (0, 0), (0, 0)))
    nb = S // Q_BLOCK

    def block(args):
        n, q = args
        start = n * Q_BLOCK
        kb = lax.dynamic_slice_in_dim(kp, start, band, axis=1)
        vb = lax.dynamic_slice_in_dim(vp, start, band, axis=1)
        qpos = start + jnp.arange(Q_BLOCK)
        kpos = start - pad + jnp.arange(band)
        valid = ((kpos[None, :] >= 0) & (kpos[None, :] < S)
                 & (jnp.abs(kpos[None, :] - qpos[:, None]) <= SW_WINDOW))
        s_loc = jnp.einsum('bqkgd,bskd->bkgqs', q, kb).astype(jnp.float32) * scale
        s_loc = jnp.where(valid, s_loc, -jnp.inf)
        s_ctx = jnp.einsum('bqkgd,bckd->bkgqc', q, kc).astype(jnp.float32) * scale
        p = softmax_with_sink(jnp.concatenate([s_ctx, s_loc], axis=-1)).astype(vx.dtype)
        return (jnp.einsum('bkgqc,bckd->bqkgd', p[..., :C], vc)
                + jnp.einsum('bkgqs,bskd->bqkgd', p[..., C:], vb))

    qb = qx.reshape(B, nb, Q_BLOCK, KV, G, d).swapaxes(0, 1)
    ox = lax.map(block, (jnp.arange(nb), qb))
    yx = ox.swapaxes(0, 1).reshape(B, S, H * d) @ w_o
    yc = None
    if ctx_out:
        s = jnp.einsum('bqkgd,bckd->bkgqc', qc, kc).astype(jnp.float32) * scale
        p = softmax_with_sink(s).astype(vc.dtype)
        yc = jnp.einsum('bkgqc,bckd->bqkgd', p, vc).reshape(B, C, H * d) @ w_o
    return yx, yc


def group_routed_moe(h, router_w, router_bias, w_gate, w_up, w_down):
    N = h.shape[0]
    aff = jax.nn.sigmoid((h @ router_w).astype(jnp.float32))
    sel = aff + router_bias.astype(jnp.float32)
    group_score = lax.top_k(sel.reshape(N, N_GROUPS, EXPERTS_PER_GROUP), TOP_K)[0].sum(-1)
    best_group = jnp.argmax(group_score, axis=-1)
    in_group = (jnp.arange(N_EXPERTS) // EXPERTS_PER_GROUP)[None, :] == best_group[:, None]
    _, idx = lax.top_k(jnp.where(in_group, sel, -jnp.inf), TOP_K)
    w = jnp.take_along_axis(aff, idx, axis=-1)
    w = w / jnp.sum(w, axis=-1, keepdims=True)
    gate = jnp.sum(jax.nn.one_hot(idx, N_EXPERTS, dtype=jnp.float32) * w[..., None], axis=1).astype(h.dtype)
    out = jnp.zeros_like(h)
    for e in range(N_EXPERTS):
        a = jax.nn.silu(h @ w_gate[e]) * (h @ w_up[e])
        out = out + gate[:, e:e + 1] * (a @ w_down[e])
    return out


def setup_inputs(seed: int = 0) -> dict:
    key = jax.random.key(seed)
    ks = iter(jax.random.split(key, 32))
    f32 = jnp.float32

    def nrm(shape, scale):
        return jax.random.normal(next(ks), shape, f32) * scale

    D = D_MODEL
    da_q = DA_HEADS * 2 * DA_HEAD_DIM
    na_w = NA_HEADS * NA_HEAD_DIM
    sw_q, sw_kv = SW_HEADS * SW_HEAD_DIM, SW_KV_HEADS * SW_HEAD_DIM
    return {
        "x": nrm((BATCH, SEQ, D), 1.0),
        "c": nrm((BATCH, D), 1.0),
        "ctx": nrm((BATCH, CTX_LEN, D), 1.0),
        "c_ctx": nrm((D,), 1.0),
        "cond_down": nrm((D, COND_DIM), D ** -0.5),
        "mod_w": nrm((DEPTH, COND_DIM, N_MOD * D), 0.5 * COND_DIM ** -0.5),
        "mod_b": nrm((DEPTH, N_MOD * D), 0.02),
        "norm_g": 1.0 + nrm((DEPTH, 2, D), 0.05),
        "final_norm_g": 1.0 + nrm((D,), 0.05),
        "da_w_qkv": nrm((N_A, D, 3 * da_q), D ** -0.5),
        "da_w_o": nrm((N_A, da_q, D), da_q ** -0.5),
        "da_lambda": nrm((N_A, 4, DA_HEAD_DIM), 0.1),
        "da_subln_g": 1.0 + nrm((N_A, 2 * DA_HEAD_DIM), 0.05),
        "na_w_qkv": nrm((N_B, D, 3 * na_w), D ** -0.5),
        "na_w_o": nrm((N_B, na_w, D), na_w ** -0.5),
        "na_rpb": nrm((N_B, NA_HEADS, 2 * NA_KH - 1, 2 * NA_KW - 1), 0.05),
        "sw_w_qkv": nrm((N_C, D, sw_q + 2 * sw_kv), D ** -0.5),
        "sw_w_o": nrm((N_C, sw_q, D), sw_q ** -0.5),
        "sw_sink": nrm((N_C, SW_HEADS), 1.0),
        "router_w": nrm((D, N_EXPERTS), D ** -0.5),
        "router_bias": nrm((N_EXPERTS,), 0.01),
        "moe_w_gate": nrm((DEPTH, N_EXPERTS, D, D_FF), D ** -0.5),
        "moe_w_up": nrm((DEPTH, N_EXPERTS, D, D_FF), D ** -0.5),
        "moe_w_down": nrm((DEPTH, N_EXPERTS, D_FF, D), D_FF ** -0.5),
    }


def reference(x, c, ctx, c_ctx, cond_down, mod_w, mod_b, norm_g, final_norm_g,
              da_w_qkv, da_w_o, da_lambda, da_subln_g,
              na_w_qkv, na_w_o, na_rpb,
              sw_w_qkv, sw_w_o, sw_sink,
              router_w, router_bias, moe_w_gate, moe_w_up, moe_w_down):
    B, S, D = x.shape
    C = ctx.shape[1]
    row, col = grid_positions(S)
    cond_x = jax.nn.silu(c) @ cond_down
    cond_c = jax.nn.silu(c_ctx) @ cond_down
    xc = ctx
    for i in range(DEPTH):
        last = i == DEPTH - 1
        mx = (cond_x @ mod_w[i] + mod_b[i]).reshape(B, N_MOD, 1, D)
        mc = (cond_c @ mod_w[i] + mod_b[i]).reshape(N_MOD, D)
        hx = rms_norm(x, norm_g[i, 0]) * (1.0 + mx[:, 1]) + mx[:, 0]
        hc = rms_norm(xc, norm_g[i, 0]) * (1.0 + mc[1]) + mc[0]
        kind, j = i % N_MIXERS, i // N_MIXERS
        if kind == 0:
            lam_init = 0.8 - 0.6 * math.exp(-0.3 * i)
            yx, yc = diff_attention_mixer(hx, hc, da_w_qkv[j], da_w_o[j], da_lambda[j], da_subln_g[j],
                                          lam_init, row, col, not last)
        elif kind == 1:
            yx, yc = neighbourhood_mixer(hx, hc, na_w_qkv[j], na_w_o[j], na_rpb[j], not last)
        else:
            yx, yc = window_gqa_mixer(hx, hc, sw_w_qkv[j], sw_w_o[j], sw_sink[j], row, col, not last)
        x = x + mx[:, 2] * yx
        hx = rms_norm(x, norm_g[i, 1]) * (1.0 + mx[:, 4]) + mx[:, 3]
        if last:
            y = group_routed_moe(hx.reshape(B * S, D), router_w, router_bias,
                                 moe_w_gate[i], moe_w_up[i], moe_w_down[i])
            x = x + mx[:, 5] * y.reshape(B, S, D)
        else:
            xc = xc + mc[2] * yc
            hc = rms_norm(xc, norm_g[i, 1]) * (1.0 + mc[4]) + mc[3]
            h_all = jnp.concatenate([hx.reshape(B * S, D), hc.reshape(B * C, D)], axis=0)
            y = group_routed_moe(h_all, router_w, router_bias,
                                 moe_w_gate[i], moe_w_up[i], moe_w_down[i])
            x = x + mx[:, 5] * y[:B * S].reshape(B, S, D)
            xc = xc + mc[5] * y[B * S:].reshape(B, C, D)
    return rms_norm(x, final_norm_g)
```

```python
import functools
import math

import numpy as np
import jax
import jax.numpy as jnp
from jax import lax
from jax.experimental import pallas as pl
from jax.experimental.pallas import tpu as pltpu

F32 = jnp.float32
BF16 = jnp.bfloat16
U32 = jnp.uint32
I32 = jnp.int32

GRID_W = 64
N_MIXERS = 3
N_MOD = 6
RMS_EPS = 1e-6
ROPE_THETA = 10000.0
DA_HEAD_DIM = 128
NA_HEAD_DIM = 128
NA_KH = 8
NA_KW = 16
SW_HEAD_DIM = 64
SW_KV_HEADS = 8
SW_WINDOW = 128
N_EXPERTS = 16
N_GROUPS = 4
EXPERTS_PER_GROUP = N_EXPERTS // N_GROUPS

LANES = 128
NEG_BIG = -1e30
NA_ROWS_PER_TILE = 4
MOE_ROW_TILE = 256
VMEM_MB = 56


def _cp(sem, vmem_mb=VMEM_MB, **kw):
    return pltpu.CompilerParams(dimension_semantics=sem, vmem_limit_bytes=vmem_mb << 20, **kw)


def _tile(n, pref, quantum=LANES):
    if n <= pref:
        return n
    t = (pref // quantum) * quantum
    while t > quantum and n % t:
        t -= quantum
    assert n % t == 0, (n, pref)
    return t


def _dot_t(a, b):
    return lax.dot_general(a, b, (((1,), (1,)), ((), ())), preferred_element_type=F32)


def _dot(a, b):
    return jnp.dot(a, b, preferred_element_type=F32)


def _mod_index_fn(n_lat_tiles, tiles_per_batch, n_batch):
    def f(i):
        return jnp.where(i < n_lat_tiles, i // tiles_per_batch, n_batch)
    return f


def _norm_rows(x, g):
    ms = jnp.mean(x * x, axis=-1, keepdims=True)
    return x * lax.rsqrt(ms + RMS_EPS) * g


def _norm_mod_kernel(x_ref, g_ref, mod_ref, h_ref, *, shift, scale):
    y = _norm_rows(x_ref[...], g_ref[...])
    h = y * (1.0 + mod_ref[0, scale:scale + 1, :]) + mod_ref[0, shift:shift + 1, :]
    h_ref[...] = h.astype(BF16)


def _top2_sum(a, b, c, d):
    hi1, lo1 = jnp.maximum(a, b), jnp.minimum(a, b)
    hi2, lo2 = jnp.maximum(c, d), jnp.minimum(c, d)
    return jnp.maximum(hi1, hi2) + jnp.maximum(jnp.minimum(hi1, hi2), jnp.maximum(lo1, lo2))


def _argmax_first(vals):
    best, idx = vals[0], jnp.zeros(vals[0].shape, I32)
    for k in range(1, len(vals)):
        upd = vals[k] > best
        idx = jnp.where(upd, k, idx)
        best = jnp.where(upd, vals[k], best)
    return idx, best


def _norm_router_kernel(x_ref, g_ref, mod_ref, rw_ref, rb_ref, hp_ref, eidx_ref, gw_ref, *, shift, scale):
    y = _norm_rows(x_ref[...], g_ref[...])
    h = y * (1.0 + mod_ref[0, scale:scale + 1, :]) + mod_ref[0, shift:shift + 1, :]
    hb = h.astype(BF16)
    half = h.shape[1] // 2
    bits = pltpu.bitcast(hb.astype(F32), U32)
    hp_ref[...] = (bits[:, :half] >> 16) | (bits[:, half:] & jnp.uint32(0xFFFF0000))

    logits = _dot(hb, rw_ref[...])
    lt = logits.T[:N_EXPERTS]
    aff = 1.0 / (1.0 + jnp.exp(-lt))
    sel = aff + rb_ref[...]
    sel_rows = [sel[e:e + 1] for e in range(N_EXPERTS)]
    aff_rows = [aff[e:e + 1] for e in range(N_EXPERTS)]
    epg = EXPERTS_PER_GROUP
    gscore = [_top2_sum(*sel_rows[g * epg:(g + 1) * epg]) for g in range(N_GROUPS)]
    bg, _ = _argmax_first(gscore)

    def pick(rows, k):
        out = rows[k]
        for g in range(1, N_GROUPS):
            out = jnp.where(bg == g, rows[g * epg + k], out)
        return out

    s_in = [pick(sel_rows, k) for k in range(epg)]
    a_in = [pick(aff_rows, k) for k in range(epg)]
    i0, _ = _argmax_first(s_in)
    i1, _ = _argmax_first([jnp.where(i0 == k, -jnp.inf, s_in[k]) for k in range(epg)])

    def take(rows, idx):
        out = rows[0]
        for k in range(1, epg):
            out = jnp.where(idx == k, rows[k], out)
        return out

    w0, w1 = take(a_in, i0), take(a_in, i1)
    tot = w0 + w1
    zf = jnp.zeros((6,) + w0.shape[1:], F32)
    gw_ref[...] = jnp.concatenate([w0 / tot, w1 / tot, zf], axis=0)
    eidx_ref[...] = jnp.concatenate([bg * epg + i0, bg * epg + i1, zf.astype(I32)], axis=0)


def _norm_mod(X, g, mods, shift, scale, tm, midx):
    R, D = X.shape
    return pl.pallas_call(
        functools.partial(_norm_mod_kernel, shift=shift, scale=scale),
        grid=(R // tm,),
        in_specs=[pl.BlockSpec((tm, D), lambda i: (i, 0)),
                  pl.BlockSpec((1, D), lambda i: (0, 0)),
                  pl.BlockSpec((1, N_MOD, D), lambda i: (midx(i), 0, 0))],
        out_specs=pl.BlockSpec((tm, D), lambda i: (i, 0)),
        out_shape=jax.ShapeDtypeStruct((R, D), BF16),
        compiler_params=_cp(("arbitrary",)),
        name="norm_mod",
    )(X, g.reshape(1, D), mods)


def _norm_router(X, g, mods, shift, scale, rw_pad, rb, tm, midx):
    R, D = X.shape
    return pl.pallas_call(
        functools.partial(_norm_router_kernel, shift=shift, scale=scale),
        grid=(R // tm,),
        in_specs=[pl.BlockSpec((tm, D), lambda i: (i, 0)),
                  pl.BlockSpec((1, D), lambda i: (0, 0)),
                  pl.BlockSpec((1, N_MOD, D), lambda i: (midx(i), 0, 0)),
                  pl.BlockSpec((D, LANES), lambda i: (0, 0)),
                  pl.BlockSpec((N_EXPERTS, 1), lambda i: (0, 0))],
        out_specs=[pl.BlockSpec((tm, D // 2), lambda i: (i, 0)),
                   pl.BlockSpec((8, tm), lambda i: (0, i)),
                   pl.BlockSpec((8, tm), lambda i: (0, i))],
        out_shape=[jax.ShapeDtypeStruct((R, D // 2), U32),
                   jax.ShapeDtypeStruct((8, R), I32),
                   jax.ShapeDtypeStruct((8, R), F32)],
        compiler_params=_cp(("arbitrary",)),
        name="norm_router",
    )(X, g.reshape(1, D), mods, rw_pad, rb)


def _final_norm_kernel(x_ref, g_ref, o_ref):
    o_ref[...] = _norm_rows(x_ref[...], g_ref[...])


def _final_norm(X, g, n_rows, tm):
    D = X.shape[1]
    return pl.pallas_call(
        _final_norm_kernel,
        grid=(n_rows // tm,),
        in_specs=[pl.BlockSpec((tm, D), lambda i: (i, 0)),
                  pl.BlockSpec((1, D), lambda i: (0, 0))],
        out_specs=pl.BlockSpec((tm, D), lambda i: (i, 0)),
        out_shape=jax.ShapeDtypeStruct((n_rows, D), F32),
        compiler_params=_cp(("arbitrary",)),
        name="final_norm",
    )(X, g.reshape(1, D))


def _mm_acc(a_ref, w_ref, wbf_ref):
    @pl.when(pl.program_id(1) == 0)
    def _():
        wbf_ref[...] = w_ref[...].astype(BF16)
    return _dot(a_ref[...], wbf_ref[...])


def _mm_bias_kernel(a_ref, w_ref, b_ref, o_ref, wbf_ref):
    o_ref[...] = (_mm_acc(a_ref, w_ref, wbf_ref) + b_ref[...]).astype(o_ref.dtype)


def _mm_res_kernel(a_ref, w_ref, r_ref, mod_ref, o_ref, wbf_ref, *, midx):
    acc = _mm_acc(a_ref, w_ref, wbf_ref)
    o_ref[...] = r_ref[...] + mod_ref[0, midx:midx + 1, :] * acc


def _mm_qkv_kernel(a_ref, w_ref, tab_ref, o_ref, wbf_ref, *, n_rope_tiles, n_q_tiles, q_scale, quarter):
    acc = _mm_acc(a_ref, w_ref, wbf_ref)
    j = pl.program_id(0)
    nblk = o_ref.shape[0]

    def store(fn):
        for b in range(nblk):
            o_ref[b] = fn(acc[:, b * LANES:(b + 1) * LANES]).astype(BF16)

    if n_rope_tiles:
        @pl.when(j < n_rope_tiles)
        def _():
            c, s1, s2 = tab_ref[0, 0], tab_ref[0, 1], tab_ref[0, 2]
            store(lambda x: x * c + pltpu.roll(x, LANES - quarter, 1) * s1 + pltpu.roll(x, quarter, 1) * s2)

        @pl.when(j >= n_rope_tiles)
        def _():
            store(lambda x: x)
    else:
        @pl.when(j < n_q_tiles)
        def _():
            store(lambda x: x * q_scale)

        @pl.when(j >= n_q_tiles)
        def _():
            store(lambda x: x)


def _mm_bias(a, w3, layer, bias, out_dtype, tn_pref=2048):
    M, K = a.shape
    N = w3.shape[2]
    tn = _tile(N, tn_pref)
    return pl.pallas_call(
        _mm_bias_kernel,
        grid=(N // tn, 1),
        in_specs=[pl.BlockSpec((M, K), lambda j, i: (0, 0)),
                  pl.BlockSpec((None, K, tn), lambda j, i: (layer, 0, j)),
                  pl.BlockSpec((1, tn), lambda j, i: (0, j))],
        out_specs=pl.BlockSpec((M, tn), lambda j, i: (0, j)),
        out_shape=jax.ShapeDtypeStruct((M, N), out_dtype),
        scratch_shapes=[pltpu.VMEM((K, tn), BF16)],
        compiler_params=_cp(("arbitrary", "arbitrary")),
        name="mm_bias",
    )(a, w3, bias)


def _mm_res(a, w3, layer, resid, mods, mod_k, tm, midx, tn_pref=512):
    M, K = a.shape
    N = w3.shape[2]
    tn = _tile(N, tn_pref)
    return pl.pallas_call(
        functools.partial(_mm_res_kernel, midx=mod_k),
        grid=(N // tn, M // tm),
        in_specs=[pl.BlockSpec((tm, K), lambda j, i: (i, 0)),
                  pl.BlockSpec((None, K, tn), lambda j, i: (layer, 0, j)),
                  pl.BlockSpec((tm, tn), lambda j, i: (i, j)),
                  pl.BlockSpec((1, N_MOD, tn), lambda j, i: (midx(i), 0, j))],
        out_specs=pl.BlockSpec((tm, tn), lambda j, i: (i, j)),
        out_shape=jax.ShapeDtypeStruct((M, N), F32),
        scratch_shapes=[pltpu.VMEM((K, tn), BF16)],
        input_output_aliases={2: 0},
        compiler_params=_cp(("arbitrary", "arbitrary")),
        name="mm_res",
    )(a, w3, resid, mods)


def _mm_qkv(a, w3, layer, tabs, tm, tn, n_q_tiles, n_rope_tiles, q_scale, quarter):
    M, K = a.shape
    N = w3.shape[2]
    nblk = tn // LANES
    if tabs is None:
        tabs = jnp.zeros((1, 3, 8, LANES), F32)
        tab_spec = pl.BlockSpec((1, 3, 8, LANES), lambda j, i: (0, 0, 0, 0))
    else:
        tab_spec = pl.BlockSpec((1, 3, tm, LANES), lambda j, i: (jnp.minimum(j // n_q_tiles, 1), 0, i, 0))
    return pl.pallas_call(
        functools.partial(_mm_qkv_kernel, n_rope_tiles=n_rope_tiles, n_q_tiles=n_q_tiles,
                          q_scale=q_scale, quarter=quarter),
        grid=(N // tn, M // tm),
        in_specs=[pl.BlockSpec((tm, K), lambda j, i: (i, 0)),
                  pl.BlockSpec((None, K, tn), lambda j, i: (layer, 0, j)),
                  tab_spec],
        out_specs=pl.BlockSpec((nblk, tm, LANES), lambda j, i: (j, i, 0)),
        out_shape=jax.ShapeDtypeStruct((N // LANES, M, LANES), BF16),
        scratch_shapes=[pltpu.VMEM((K, tn), BF16)],
        compiler_params=_cp(("arbitrary", "arbitrary")),
        name="mm_qkv",
    )(a, w3, tabs)


def _rope_tables(n_lat, n_batch, n_ctx_rows, head_dim, q_scale):
    half, quarter = head_dim // 2, head_dim // 4
    pos = np.arange(n_lat)
    row, col = (pos // GRID_W).astype(np.float32), (pos % GRID_W).astype(np.float32)
    lane = np.arange(LANES) % head_dim
    inv = jnp.asarray(ROPE_THETA, F32) ** (-jnp.arange(quarter, dtype=F32) / quarter)
    p = jnp.where(jnp.asarray(lane // half == 0)[None, :], jnp.asarray(row)[:, None], jnp.asarray(col)[:, None])
    ang = p * inv[np.asarray((lane % half) % quarter)][None, :]
    first = jnp.asarray((lane % half) < quarter)[None, :]
    cos, sin = jnp.cos(ang), jnp.sin(ang)
    t = jnp.stack([cos, jnp.where(first, -sin, 0.0), jnp.where(first, 0.0, sin)])
    t = jnp.tile(t, (1, n_batch, 1))
    ident = jnp.stack([jnp.ones((n_ctx_rows, LANES), F32), jnp.zeros((n_ctx_rows, LANES), F32),
                       jnp.zeros((n_ctx_rows, LANES), F32)])
    t = jnp.concatenate([t, ident], axis=1)
    return jnp.stack([t * q_scale, t])


def _da_kernel(lam_ref, q_ref, kc_ref, vc_ref, *rest, has_lat, out_scale):
    if has_lat:
        kl_ref, vl_ref, g_ref, o_ref = rest
    else:
        g_ref, _, o_ref = rest
    lam = lam_ref[0]

    def probs(m):
        q = q_ref[m]
        sc = _dot_t(q, kc_ref[m])
        mx = jnp.max(sc, axis=-1, keepdims=True)
        if has_lat:
            sl = _dot_t(q, kl_ref[m])
            mx = jnp.maximum(mx, jnp.max(sl, axis=-1, keepdims=True))
        pc = jnp.exp(sc - mx)
        l = jnp.sum(pc, axis=-1, keepdims=True)
        pl_ = None
        if has_lat:
            pl_ = jnp.exp(sl - mx)
            l = l + jnp.sum(pl_, axis=-1, keepdims=True)
        return pc, pl_, l

    pc0, pl0, l0 = probs(0)
    pc1, pl1, l1 = probs(1)
    r0 = 1.0 / l0
    r1 = lam / l1
    vc = jnp.concatenate([vc_ref[0], vc_ref[1]], axis=1)
    o = _dot((pc0 * r0 - pc1 * r1).astype(BF16), vc)
    if has_lat:
        vl = jnp.concatenate([vl_ref[0], vl_ref[1]], axis=1)
        o = o + _dot((pl0 * r0 - pl1 * r1).astype(BF16), vl)
    o_ref[...] = (_norm_rows(o, g_ref[...]) * out_scale).astype(BF16)


def _da_attention(qkv, lam, subln_g, out_scale, n_batch, S, C, H, tq):
    R = qkv.shape[1]
    n_lat = n_batch * S
    d2 = 2 * DA_HEAD_DIM
    g = subln_g.reshape(1, d2)
    smem = pl.BlockSpec(memory_space=pltpu.SMEM)
    tql = _tile(S, tq, 8)
    out = pl.pallas_call(
        functools.partial(_da_kernel, has_lat=True, out_scale=out_scale),
        grid=(n_batch, H, S // tql),
        in_specs=[smem,
                  pl.BlockSpec((2, tql, LANES), lambda b, h, i: (h, b * (S // tql) + i, 0)),
                  pl.BlockSpec((2, C, LANES), lambda b, h, i: (H + h, n_lat // C + b, 0)),
                  pl.BlockSpec((2, C, LANES), lambda b, h, i: (2 * H + h, n_lat // C + b, 0)),
                  pl.BlockSpec((2, S, LANES), lambda b, h, i: (H + h, b, 0)),
                  pl.BlockSpec((2, S, LANES), lambda b, h, i: (2 * H + h, b, 0)),
                  pl.BlockSpec((1, d2), lambda b, h, i: (0, 0))],
        out_specs=pl.BlockSpec((tql, d2), lambda b, h, i: (b * (S // tql) + i, h)),
        out_shape=jax.ShapeDtypeStruct((R, H * d2), BF16),
        compiler_params=_cp(("arbitrary", "arbitrary", "arbitrary")),
        name="da_attn_lat",
    )(lam, qkv, qkv, qkv, qkv, qkv, g)
    tqc = _tile(C, tq, 8)
    return pl.pallas_call(
        functools.partial(_da_kernel, has_lat=False, out_scale=out_scale),
        grid=(n_batch, H, C // tqc),
        in_specs=[smem,
                  pl.BlockSpec((2, tqc, LANES), lambda b, h, i: (h, (n_lat + b * C) // tqc + i, 0)),
                  pl.BlockSpec((2, C, LANES), lambda b, h, i: (H + h, n_lat // C + b, 0)),
                  pl.BlockSpec((2, C, LANES), lambda b, h, i: (2 * H + h, n_lat // C + b, 0)),
                  pl.BlockSpec((1, d2), lambda b, h, i: (0, 0)),
                  pl.BlockSpec(memory_space=pl.ANY)],
        out_specs=pl.BlockSpec((tqc, d2), lambda b, h, i: ((n_lat + b * C) // tqc + i, h)),
        out_shape=jax.ShapeDtypeStruct((R, H * d2), BF16),
        input_output_aliases={5: 0},
        compiler_params=_cp(("arbitrary", "arbitrary", "arbitrary")),
        name="da_attn_ctx",
    )(lam, qkv, qkv, qkv, g, out)


def _na_geometry(rows):
    kh, rq = min(NA_KH, rows), NA_ROWS_PER_TILE
    kr = kh + rq
    assert rows % rq == 0 and rows >= kr and rows // rq >= 3
    return kh, rq, kr


def _na_bias(rpb, rows):
    kh, rq, kr = _na_geometry(rows)
    W, kw = GRID_W, NA_KW
    n_tiles = rows // rq

    def row_geometry(i):
        r = i * rq + np.arange(rq)
        k = int(np.clip(i * rq - kh // 2, 0, rows - kr)) + np.arange(kr)
        rs = np.clip(r - kh // 2, 0, rows - kh)
        valid = (k[None, :] >= rs[:, None]) & (k[None, :] < rs[:, None] + kh)
        return valid, np.where(valid, k[None, :] - r[:, None] + (NA_KH - 1), 0)

    pats = [row_geometry(0), row_geometry(1), row_geometry(n_tiles - 1)]
    for i in range(1, n_tiles - 1):
        v, d = row_geometry(i)
        assert (v == pats[1][0]).all() and (d == pats[1][1]).all()
    vrow = np.stack([p[0] for p in pats])
    drow = np.stack([p[1] for p in pats])
    c = np.arange(W)
    cs = np.clip(c - kw // 2, 0, W - kw)
    vcol = (c[None, :] >= cs[:, None]) & (c[None, :] < cs[:, None] + kw)
    dcol = np.where(vcol, c[None, :] - c[:, None] + (NA_KW - 1), 0)
    b = rpb[:, drow[:, :, None, :, None], dcol[None, None, :, None, :]]
    valid = vrow[:, :, None, :, None] & vcol[None, None, :, None, :]
    b = jnp.where(valid[None], b.astype(F32), NEG_BIG)
    H = rpb.shape[0]
    return jnp.transpose(b, (1, 0, 2, 3, 4, 5)).reshape(3, H, rq * W, kr * W)


def _na_kernel(q_ref, kc_ref, vc_ref, *rest, has_loc, rows):
    if has_loc:
        k_ref, v_ref, bias_ref, o_ref = rest
    else:
        _, o_ref = rest
    q = q_ref[0]
    sc = _dot_t(q, kc_ref[0])
    mx = jnp.max(sc, axis=-1, keepdims=True)
    if has_loc:
        kh, rq, kr = _na_geometry(rows)
        start = jnp.clip(pl.program_id(2) * rq - kh // 2, 0, rows - kr) * GRID_W
        start = pl.multiple_of(start, GRID_W)
        sl = _dot_t(q, k_ref[0, pl.ds(start, kr * GRID_W), :]) + bias_ref[0, 0]
        mx = jnp.maximum(mx, jnp.max(sl, axis=-1, keepdims=True))
    pc = jnp.exp(sc - mx)
    l = jnp.sum(pc, axis=-1, keepdims=True)
    o = _dot(pc.astype(BF16), vc_ref[0])
    if has_loc:
        pl_ = jnp.exp(sl - mx)
        l = l + jnp.sum(pl_, axis=-1, keepdims=True)
        o = o + _dot(pl_.astype(BF16), v_ref[0, pl.ds(start, kr * GRID_W), :])
    o_ref[...] = (o * (1.0 / l)).astype(BF16)


def _na_attention(qkv, bias, n_batch, S, C, H):
    R = qkv.shape[1]
    n_lat = n_batch * S
    rows = S // GRID_W
    kh, rq, kr = _na_geometry(rows)
    tq, n_loc, n_tiles = rq * GRID_W, kr * GRID_W, rows // rq
    d = NA_HEAD_DIM

    def pat(i):
        return jnp.where(i == 0, 0, jnp.where(i == n_tiles - 1, 2, 1))

    out = pl.pallas_call(
        functools.partial(_na_kernel, has_loc=True, rows=rows),
        grid=(n_batch, H, n_tiles),
        in_specs=[pl.BlockSpec((1, tq, LANES), lambda b, h, i: (h, b * n_tiles + i, 0)),
                  pl.BlockSpec((1, C, LANES), lambda b, h, i: (H + h, n_lat // C + b, 0)),
                  pl.BlockSpec((1, C, LANES), lambda b, h, i: (2 * H + h, n_lat // C + b, 0)),
                  pl.BlockSpec((1, S, LANES), lambda b, h, i: (H + h, b, 0)),
                  pl.BlockSpec((1, S, LANES), lambda b, h, i: (2 * H + h, b, 0)),
                  pl.BlockSpec((1, 1, tq, n_loc), lambda b, h, i: (pat(i), h, 0, 0))],
        out_specs=pl.BlockSpec((tq, d), lambda b, h, i: (b * n_tiles + i, h)),
        out_shape=jax.ShapeDtypeStruct((R, H * d), BF16),
        compiler_params=_cp(("arbitrary", "arbitrary", "arbitrary")),
        name="na_attn_lat",
    )(qkv, qkv, qkv, qkv, qkv, bias)
    return pl.pallas_call(
        functools.partial(_na_kernel, has_loc=False, rows=rows),
        grid=(n_batch, H, 1),
        in_specs=[pl.BlockSpec((1, C, LANES), lambda b, h, i: (h, n_lat // C + b, 0)),
                  pl.BlockSpec((1, C, LANES), lambda b, h, i: (H + h, n_lat // C + b, 0)),
                  pl.BlockSpec((1, C, LANES), lambda b, h, i: (2 * H + h, n_lat // C + b, 0)),
                  pl.BlockSpec(memory_space=pl.ANY)],
        out_specs=pl.BlockSpec((C, d), lambda b, h, i: (n_lat // C + b, h)),
        out_shape=jax.ShapeDtypeStruct((R, H * d), BF16),
        input_output_aliases={3: 0},
        compiler_params=_cp(("arbitrary", "arbitrary", "arbitrary")),
        name="na_attn_ctx",
    )(qkv, qkv, qkv, out)


def _sw_kernel(sink_ref, q_ref, kc_ref, vc_ref, *rest, has_loc, tq, band, S, G):
    if has_loc:
        k_ref, v_ref, o_ref = rest
    else:
        _, o_ref = rest
    hd = SW_HEAD_DIM
    per_blk = LANES // hd
    g2n = G // per_blk
    C = kc_ref.shape[1]
    c, i = pl.program_id(1), pl.program_id(2)
    lane = lax.broadcasted_iota(I32, (1, LANES), 1)

    def block_diag(x, par):
        sw = jnp.concatenate([x[:, hd:], x[:, :hd]], axis=1)
        zero = jnp.zeros_like(x)
        top = jnp.where(lane < hd, x if par == 0 else sw, zero)
        bot = jnp.where(lane >= hd, sw if par == 0 else x, zero)
        return jnp.concatenate([top, bot], axis=0)

    if has_loc:
        bstart = pl.multiple_of(jnp.clip(i * tq - SW_WINDOW, 0, S - band), LANES)
        kb = k_ref[0, pl.ds(bstart, band), :]
        vb = v_ref[0, pl.ds(bstart, band), :]
        qpos = i * tq + lax.broadcasted_iota(I32, (tq, 1), 0)
        kpos = bstart + lax.broadcasted_iota(I32, (1, band), 1)
        valid = jnp.abs(kpos - qpos) <= SW_WINDOW

    for par in range(2):
        kc_bd, vc_bd = block_diag(kc_ref[0], par), block_diag(vc_ref[0], par)
        if has_loc:
            kl_bd, vl_bd = block_diag(kb, par), block_diag(vb, par)
        for g2 in range(g2n):
            blk = par * g2n + g2
            q = q_ref[blk]
            sc2 = _dot_t(q, kc_bd)
            if has_loc:
                sl2 = _dot_t(q, kl_bd)
            pcs, pls, rls = [], [], []
            for hh in range(per_blk):
                sink = sink_ref[(2 * c + par) * G + g2 * per_blk + hh]
                sc = sc2[:, hh * C:(hh + 1) * C]
                mx = jnp.maximum(jnp.max(sc, axis=-1, keepdims=True), sink)
                if has_loc:
                    sl = jnp.where(valid, sl2[:, hh * band:(hh + 1) * band], NEG_BIG)
                    mx = jnp.maximum(mx, jnp.max(sl, axis=-1, keepdims=True))
                pc = jnp.exp(sc - mx)
                l = jnp.sum(pc, axis=-1, keepdims=True) + jnp.exp(sink - mx)
                pcs.append(pc)
                if has_loc:
                    pl_ = jnp.exp(sl - mx)
                    l = l + jnp.sum(pl_, axis=-1, keepdims=True)
                    pls.append(pl_)
                rls.append(1.0 / l)
            o = _dot(jnp.concatenate(pcs, axis=1).astype(BF16), vc_bd)
            if has_loc:
                o = o + _dot(jnp.concatenate(pls, axis=1).astype(BF16), vl_bd)
            o = o * jnp.where(lane < hd, rls[0], rls[1])
            o_ref[:, blk * LANES:(blk + 1) * LANES] = o.astype(BF16)


def _sw_attention(qkv, sink, n_batch, S, C, H, tq):
    R = qkv.shape[1]
    n_lat = n_batch * S
    hd, KV = SW_HEAD_DIM, SW_KV_HEADS
    G = H // KV
    nqb, nkb = H * hd // LANES, KV * hd // LANES
    qpk = 2 * G * hd // LANES
    assert KV % 2 == 0 and LANES // hd == 2 and G % 2 == 0
    tq = _tile(S, tq)
    band = tq + 2 * SW_WINDOW
    assert band <= S and SW_WINDOW % LANES == 0
    smem = pl.BlockSpec(memory_space=pltpu.SMEM)
    out = pl.pallas_call(
        functools.partial(_sw_kernel, has_loc=True, tq=tq, band=band, S=S, G=G),
        grid=(n_batch, KV // 2, S // tq),
        in_specs=[smem,
                  pl.BlockSpec((qpk, tq, LANES), lambda b, c, i: (c, b * (S // tq) + i, 0)),
                  pl.BlockSpec((1, C, LANES), lambda b, c, i: (nqb + c, n_lat // C + b, 0)),
                  pl.BlockSpec((1, C, LANES), lambda b, c, i: (nqb + nkb + c, n_lat // C + b, 0)),
                  pl.BlockSpec((1, S, LANES), lambda b, c, i: (nqb + c, b, 0)),
                  pl.BlockSpec((1, S, LANES), lambda b, c, i: (nqb + nkb + c, b, 0))],
        out_specs=pl.BlockSpec((tq, qpk * LANES), lambda b, c, i: (b * (S // tq) + i, c)),
        out_shape=jax.ShapeDtypeStruct((R, H * hd), BF16),
        compiler_params=_cp(("arbitrary", "arbitrary", "arbitrary")),
        name="sw_attn_lat",
    )(sink, qkv, qkv, qkv, qkv, qkv)
    return pl.pallas_call(
        functools.partial(_sw_kernel, has_loc=False, tq=C, band=0, S=S, G=G),
        grid=(n_batch, KV // 2, 1),
        in_specs=[smem,
                  pl.BlockSpec((qpk, C, LANES), lambda b, c, i: (c, n_lat // C + b, 0)),
                  pl.BlockSpec((1, C, LANES), lambda b, c, i: (nqb + c, n_lat // C + b, 0)),
                  pl.BlockSpec((1, C, LANES), lambda b, c, i: (nqb + nkb + c, n_lat // C + b, 0)),
                  pl.BlockSpec(memory_space=pl.ANY)],
        out_specs=pl.BlockSpec((C, qpk * LANES), lambda b, c, i: (n_lat // C + b, c)),
        out_shape=jax.ShapeDtypeStruct((R, H * hd), BF16),
        input_output_aliases={4: 0},
        compiler_params=_cp(("arbitrary", "arbitrary", "arbitrary")),
        name="sw_attn_ctx",
    )(sink, qkv, qkv, qkv, out)


def _moe_plan(eidx, gw, tme, p_max):
    R = eidx.shape[1]
    e_flat = eidx[:2].reshape(-1)
    w_flat = gw[:2].reshape(-1)
    onehot = (e_flat[:, None] == jnp.arange(N_EXPERTS, dtype=I32)[None, :]).astype(I32)
    csum = jnp.cumsum(onehot, axis=0)
    rank = jnp.sum(csum * onehot, axis=1) - 1
    cnt = csum[-1]
    pcnt = (cnt + tme - 1) // tme * tme
    ends = jnp.cumsum(pcnt)
    offs = ends - pcnt
    dest = (offs[e_flat] + rank).astype(I32)
    tok = jnp.tile(jnp.arange(R, dtype=I32), 2)
    src = jnp.zeros((p_max,), I32).at[dest].set(tok)
    gate = jnp.zeros((p_max,), F32).at[dest].set(w_flat)
    n_tiles = (ends[-1] // tme).astype(I32).reshape(1)
    return dest, src, gate.reshape(p_max, 1), (pcnt // tme).astype(I32), (offs // tme).astype(I32), n_tiles


def _work_list(tiles_e, tile_base, n_chunks, n_items_max):
    items_e = tiles_e * n_chunks
    ends = jnp.cumsum(items_e)
    nw = ends[-1]
    w = jnp.minimum(jnp.arange(n_items_max, dtype=I32), nw - 1)
    e = jnp.sum((w[:, None] >= ends[None, :]).astype(I32), axis=1)
    local = w - (ends[e] - items_e[e])
    te = jnp.maximum(tiles_e[e], 1)
    chunk, r = local // te, local % te
    return (e.astype(I32), chunk.astype(I32), (tile_base[e] + r).astype(I32),
            (r == 0).astype(I32), nw.astype(I32).reshape(1))


def _row_copy(src_hbm, src_row, dst, dst_row, sem):
    return pltpu.make_async_copy(src_hbm.at[pl.ds(src_row, 1)], dst.at[pl.ds(dst_row, 1)], sem)


def _gather_rows_kernel(src_ref, nt_ref, h_hbm, o_hbm, sem, *, tme):
    t = pl.program_id(0)

    @pl.when(t < nt_ref[0])
    def _():
        base = t * tme

        def issue(r, carry):
            _row_copy(h_hbm, src_ref[base + r], o_hbm, base + r, sem).start()
            return carry

        def drain(r, carry):
            _row_copy(h_hbm, 0, o_hbm, base + r, sem).wait()
            return carry

        lax.fori_loop(0, tme, issue, 0)
        lax.fori_loop(0, tme, drain, 0)


def _gather_rows(hp, src, n_tiles, tme, p_max):
    return pl.pallas_call(
        functools.partial(_gather_rows_kernel, tme=tme),
        grid_spec=pltpu.PrefetchScalarGridSpec(
            num_scalar_prefetch=2, grid=(p_max // tme,),
            in_specs=[pl.BlockSpec(memory_space=pl.ANY)],
            out_specs=pl.BlockSpec(memory_space=pl.ANY),
            scratch_shapes=[pltpu.SemaphoreType.DMA(())]),
        out_shape=jax.ShapeDtypeStruct((p_max, hp.shape[1]), hp.dtype),
        compiler_params=_cp(("arbitrary",)),
        name="moe_gather",
    )(src, n_tiles, hp)


def _silu(g):
    return g / (1.0 + jnp.exp(-g))


def _moe_up_kernel(ie, ic, it, ifirst, nw, hp_ref, wg_ref, wu_ref, a_ref, wgb, wub):
    w = pl.program_id(0)

    @pl.when(w < nw[0])
    def _():
        @pl.when(ifirst[w] == 1)
        def _():
            wgb[...] = wg_ref[...].astype(BF16)
            wub[...] = wu_ref[...].astype(BF16)

        hp = hp_ref[...]
        half = hp.shape[1]
        lo = pltpu.bitcast(hp << 16, F32).astype(BF16)
        hi = pltpu.bitcast(hp & jnp.uint32(0xFFFF0000), F32).astype(BF16)
        g = _dot(lo, wgb[:half]) + _dot(hi, wgb[half:])
        u = _dot(lo, wub[:half]) + _dot(hi, wub[half:])
        a_ref[...] = (_silu(g) * u).astype(BF16)


def _moe_down_kernel(ie, ic, it, ifirst, nw, a_ref, wd_ref, gate_ref, y_ref, wdb):
    w = pl.program_id(0)

    @pl.when(w < nw[0])
    def _():
        @pl.when(ifirst[w] == 1)
        def _():
            wdb[...] = wd_ref[...].astype(BF16)

        y_ref[...] = gate_ref[...] * _dot(a_ref[...], wdb[...])


def _combine_kernel(dest_ref, y_hbm, x_ref, mod_ref, o_ref, buf, sem, *, tc, n_rows, mod_k):
    base = pl.program_id(0) * tc

    def issue(r, carry):
        for k in range(2):
            _row_copy(y_hbm, dest_ref[k * n_rows + base + r], buf.at[k], r, sem).start()
        return carry

    def drain(r, carry):
        for k in range(2):
            _row_copy(y_hbm, 0, buf.at[k], r, sem).wait()
        return carry

    lax.fori_loop(0, tc, issue, 0)
    lax.fori_loop(0, tc, drain, 0)
    o_ref[...] = x_ref[...] + mod_ref[0, mod_k:mod_k + 1, :] * (buf[0] + buf[1])


def _moe(X, hp, eidx, gw, w_gate, w_up, w_down, layer, mods, mod_k, midx):
    R, D = X.shape
    d_ff = w_gate.shape[3]
    tme = MOE_ROW_TILE
    p_max = -(-(2 * R + N_EXPERTS * (tme - 1)) // tme) * tme
    t_max = p_max // tme
    dest, src, gate, tiles_e, tile_base, n_tiles = _moe_plan(eidx, gw, tme, p_max)
    hs = _gather_rows(hp, src, n_tiles, tme, p_max)

    fc = _tile(d_ff, 384)
    n_f = d_ff // fc
    items = _work_list(tiles_e, tile_base, n_f, n_f * t_max)
    a = pl.pallas_call(
        _moe_up_kernel,
        grid_spec=pltpu.PrefetchScalarGridSpec(
            num_scalar_prefetch=5, grid=(n_f * t_max,),
            in_specs=[pl.BlockSpec((tme, D // 2), lambda w, ie, ic, it, fi, nw: (it[w], 0)),
                      pl.BlockSpec((None, None, D, fc), lambda w, ie, ic, it, fi, nw: (layer, ie[w], 0, ic[w])),
                      pl.BlockSpec((None, None, D, fc), lambda w, ie, ic, it, fi, nw: (layer, ie[w], 0, ic[w]))],
            out_specs=pl.BlockSpec((tme, fc), lambda w, ie, ic, it, fi, nw: (it[w], ic[w])),
            scratch_shapes=[pltpu.VMEM((D, fc), BF16), pltpu.VMEM((D, fc), BF16)]),
        out_shape=jax.ShapeDtypeStruct((p_max, d_ff), BF16),
        compiler_params=_cp(("arbitrary",)),
        name="moe_up",
    )(*items, hs, w_gate, w_up)

    nc = _tile(D, 2048)
    n_n = D // nc
    items = _work_list(tiles_e, tile_base, n_n, n_n * t_max)
    y = pl.pallas_call(
        _moe_down_kernel,
        grid_spec=pltpu.PrefetchScalarGridSpec(
            num_scalar_prefetch=5, grid=(n_n * t_max,),
            in_specs=[pl.BlockSpec((tme, d_ff), lambda w, ie, ic, it, fi, nw: (it[w], 0)),
                      pl.BlockSpec((None, None, d_ff, nc), lambda w, ie, ic, it, fi, nw: (layer, ie[w], 0, ic[w])),
                      pl.BlockSpec((tme, 1), lambda w, ie, ic, it, fi, nw: (it[w], 0))],
            out_specs=pl.BlockSpec((tme, nc), lambda w, ie, ic, it, fi, nw: (it[w], ic[w])),
            scratch_shapes=[pltpu.VMEM((d_ff, nc), BF16)]),
        out_shape=jax.ShapeDtypeStruct((p_max, D), F32),
        compiler_params=_cp(("arbitrary",)),
        name="moe_down",
    )(*items, a, w_down, gate)

    tc = _tile(R, 128, 8)
    n_lat_tiles_fn = midx
    return pl.pallas_call(
        functools.partial(_combine_kernel, tc=tc, n_rows=R, mod_k=mod_k),
        grid_spec=pltpu.PrefetchScalarGridSpec(
            num_scalar_prefetch=1, grid=(R // tc,),
            in_specs=[pl.BlockSpec(memory_space=pl.ANY),
                      pl.BlockSpec((tc, D), lambda i, d: (i, 0)),
                      pl.BlockSpec((1, N_MOD, D), lambda i, d: (n_lat_tiles_fn(tc)(i), 0, 0))],
            out_specs=pl.BlockSpec((tc, D), lambda i, d: (i, 0)),
            scratch_shapes=[pltpu.VMEM((2, tc, D), F32), pltpu.SemaphoreType.DMA(())]),
        out_shape=jax.ShapeDtypeStruct((R, D), F32),
        input_output_aliases={2: 0},
        compiler_params=_cp(("arbitrary",)),
        name="moe_combine",
    )(dest, y, X, mods)


def kernel(x, c, ctx, c_ctx, cond_down, mod_w, mod_b, norm_g, final_norm_g, da_w_qkv, da_w_o, da_lambda, da_subln_g, na_w_qkv, na_w_o, na_rpb, sw_w_qkv, sw_w_o, sw_sink, router_w, router_bias, moe_w_gate, moe_w_up, moe_w_down):
    B, S, D = x.shape
    C = ctx.shape[1]
    depth = mod_w.shape[0]
    n_lat, n_ctx = B * S, B * C
    R = n_lat + n_ctx

    def midx(tm):
        assert S % tm == 0 and n_ctx % tm == 0
        return _mod_index_fn(n_lat // tm, S // tm, B)

    X = jnp.concatenate([x.reshape(n_lat, D), ctx.reshape(n_ctx, D)], axis=0)

    cond_rows = 16
    cin = jnp.concatenate([c, c_ctx[None, :], jnp.zeros((cond_rows - B - 1, D), F32)], axis=0)
    cin = jax.nn.silu(cin).astype(BF16)
    cond = _mm_bias(cin, cond_down[None], 0, jnp.zeros((1, cond_down.shape[1]), F32), BF16)

    rw_pad = jnp.zeros((D, LANES), F32).at[:, :N_EXPERTS].set(router_w).astype(BF16)
    rb = router_bias.astype(F32).reshape(N_EXPERTS, 1)

    tm_row = _tile(math.gcd(S, n_ctx), 256, 8)
    tm_mm = _tile(math.gcd(S, n_ctx), 512, 8)
    da_h = da_w_qkv.shape[2] // (6 * DA_HEAD_DIM)
    na_h = na_w_qkv.shape[2] // (3 * NA_HEAD_DIM)
    sw_h = sw_sink.shape[1]
    tabs = {}

    for i in range(depth):
        mods = _mm_bias(cond, mod_w, i, mod_b[i][None, :], F32)[:B + 1].reshape(B + 1, N_MOD, D)
        h = _norm_mod(X, norm_g[i, 0], mods, 0, 1, tm_row, midx(tm_row))
        kind, j = i % N_MIXERS, i // N_MIXERS
        if kind == 0:
            d = DA_HEAD_DIM
            qd = da_h * 2 * d
            tn = _tile(qd, 512)
            if "da" not in tabs:
                tabs["da"] = _rope_tables(S, B, n_ctx, d, d ** -0.5)
            qkv = _mm_qkv(h, da_w_qkv, j, tabs["da"], tm_mm, tn, qd // tn, 2 * qd // tn, 1.0, d // 4)
            lam_init = 0.8 - 0.6 * math.exp(-0.3 * i)
            lp = da_lambda[j].astype(F32)
            lam = (jnp.exp(jnp.sum(lp[0] * lp[1])) - jnp.exp(jnp.sum(lp[2] * lp[3])) + lam_init).reshape(1)
            o = _da_attention(qkv, lam, da_subln_g[j], 1.0 - lam_init, B, S, C, da_h, 256)
            w_o = da_w_o
        elif kind == 1:
            d = NA_HEAD_DIM
            tn = _tile(na_h * d, 512)
            qkv = _mm_qkv(h, na_w_qkv, j, None, tm_mm, tn, na_h * d // tn, 0, d ** -0.5, 0)
            o = _na_attention(qkv, _na_bias(na_rpb[j], S // GRID_W), B, S, C, na_h)
            w_o = na_w_o
        else:
            d = SW_HEAD_DIM
            nq, nkv = sw_h * d, SW_KV_HEADS * d
            tn = _tile(math.gcd(nq, nkv), 512)
            if "sw" not in tabs:
                tabs["sw"] = _rope_tables(S, B, n_ctx, d, d ** -0.5)
            qkv = _mm_qkv(h, sw_w_qkv, j, tabs["sw"], tm_mm, tn, nq // tn, (nq + nkv) // tn, 1.0, d // 4)
            o = _sw_attention(qkv, sw_sink[j].astype(F32), B, S, C, sw_h, 256)
            w_o = sw_w_o
        X = _mm_res(o, w_o, j, X, mods, 2, tm_mm, midx(tm_mm))
        hp, eidx, gw = _norm_router(X, norm_g[i, 1], mods, 3, 4, rw_pad, rb, tm_row, midx(tm_row))
        X = _moe(X, hp, eidx, gw, moe_w_gate, moe_w_up, moe_w_down, i, mods, 5, midx)

    tm_fin = _tile(S, 256, 8)
    return _final_norm(X, final_norm_g, n_lat, tm_fin).reshape(B, S, D)
```

```python
import functools
import math

import numpy as np
import jax
import jax.numpy as jnp
from jax import lax
from jax.experimental import pallas as pl
from jax.experimental.pallas import tpu as pltpu

F32 = jnp.float32
BF16 = jnp.bfloat16
U32 = jnp.uint32
I32 = jnp.int32

GRID_W = 64
N_MIXERS = 3
N_MOD = 6
RMS_EPS = 1e-6
ROPE_THETA = 10000.0
DA_HEAD_DIM = 128
NA_HEAD_DIM = 128
NA_KH = 8
NA_KW = 16
SW_HEAD_DIM = 64
SW_KV_HEADS = 8
SW_WINDOW = 128
N_EXPERTS = 16
N_GROUPS = 4
EXPERTS_PER_GROUP = N_EXPERTS // N_GROUPS

LANES = 128
NEG_BIG = -1e30
NA_ROWS_PER_TILE = 4
MOE_ROW_TILE = 256
VMEM_MB = 56


def _cp(sem, vmem_mb=VMEM_MB, **kw):
    return pltpu.CompilerParams(dimension_semantics=sem, vmem_limit_bytes=vmem_mb << 20, **kw)


def _tile(n, pref, quantum=LANES):
    if n <= pref:
        return n
    t = (pref // quantum) * quantum
    while t > quantum and n % t:
        t -= quantum
    assert n % t == 0, (n, pref)
    return t


def _dot_t(a, b):
    return lax.dot_general(a, b, (((1,), (1,)), ((), ())), preferred_element_type=F32)


def _dot(a, b):
    return jnp.dot(a, b, preferred_element_type=F32)


def _mod_index_fn(n_lat_tiles, tiles_per_batch, n_batch):
    def f(i):
        return jnp.where(i < n_lat_tiles, i // tiles_per_batch, n_batch)
    return f


def _norm_rows(x, g):
    ms = jnp.mean(x * x, axis=-1, keepdims=True)
    return x * lax.rsqrt(ms + RMS_EPS) * g


def _norm_mod_kernel(x_ref, g_ref, mod_ref, h_ref, *, shift, scale):
    y = _norm_rows(x_ref[...], g_ref[...])
    h = y * (1.0 + mod_ref[0, scale:scale + 1, :]) + mod_ref[0, shift:shift + 1, :]
    h_ref[...] = h.astype(BF16)


def _top2_sum(a, b, c, d):
    hi1, lo1 = jnp.maximum(a, b), jnp.minimum(a, b)
    hi2, lo2 = jnp.maximum(c, d), jnp.minimum(c, d)
    return jnp.maximum(hi1, hi2) + jnp.maximum(jnp.minimum(hi1, hi2), jnp.maximum(lo1, lo2))


def _argmax_first(vals):
    best, idx = vals[0], jnp.zeros(vals[0].shape, I32)
    for k in range(1, len(vals)):
        upd = vals[k] > best
        idx = jnp.where(upd, k, idx)
        best = jnp.where(upd, vals[k], best)
    return idx, best


def _norm_router_kernel(x_ref, g_ref, mod_ref, rw_ref, rb_ref, hp_ref, eidx_ref, gw_ref, *, shift, scale):
    y = _norm_rows(x_ref[...], g_ref[...])
    h = y * (1.0 + mod_ref[0, scale:scale + 1, :]) + mod_ref[0, shift:shift + 1, :]
    hb = h.astype(BF16)
    half = h.shape[1] // 2
    bits = pltpu.bitcast(hb.astype(F32), U32)
    hp_ref[...] = (bits[:, :half] >> 16) | (bits[:, half:] & jnp.uint32(0xFFFF0000))

    logits = _dot(hb, rw_ref[...])
    lt = logits.T[:N_EXPERTS]
    aff = 1.0 / (1.0 + jnp.exp(-lt))
    sel = aff + rb_ref[...]
    sel_rows = [sel[e:e + 1] for e in range(N_EXPERTS)]
    aff_rows = [aff[e:e + 1] for e in range(N_EXPERTS)]
    epg = EXPERTS_PER_GROUP
    gscore = [_top2_sum(*sel_rows[g * epg:(g + 1) * epg]) for g in range(N_GROUPS)]
    bg, _ = _argmax_first(gscore)

    def pick(rows, k):
        out = rows[k]
        for g in range(1, N_GROUPS):
            out = jnp.where(bg == g, rows[g * epg + k], out)
        return out

    s_in = [pick(sel_rows, k) for k in range(epg)]
    a_in = [pick(aff_rows, k) for k in range(epg)]
    i0, _ = _argmax_first(s_in)
    i1, _ = _argmax_first([jnp.where(i0 == k, -jnp.inf, s_in[k]) for k in range(epg)])

    def take(rows, idx):
        out = rows[0]
        for k in range(1, epg):
            out = jnp.where(idx == k, rows[k], out)
        return out

    w0, w1 = take(a_in, i0), take(a_in, i1)
    tot = w0 + w1
    zf = jnp.zeros((6,) + w0.shape[1:], F32)
    gw_ref[...] = jnp.concatenate([w0 / tot, w1 / tot, zf], axis=0)
    eidx_ref[...] = jnp.concatenate([bg * epg + i0, bg * epg + i1, zf.astype(I32)], axis=0)


def _norm_mod(X, g, mods, shift, scale, tm, midx):
    R, D = X.shape
    return pl.pallas_call(
        functools.partial(_norm_mod_kernel, shift=shift, scale=scale),
        grid=(R // tm,),
        in_specs=[pl.BlockSpec((tm, D), lambda i: (i, 0)),
                  pl.BlockSpec((1, D), lambda i: (0, 0)),
                  pl.BlockSpec((1, N_MOD, D), lambda i: (midx(i), 0, 0))],
        out_specs=pl.BlockSpec((tm, D), lambda i: (i, 0)),
        out_shape=jax.ShapeDtypeStruct((R, D), BF16),
        compiler_params=_cp(("arbitrary",)),
        name="norm_mod",
    )(X, g.reshape(1, D), mods)


def _norm_router(X, g, mods, shift, scale, rw_pad, rb, tm, midx):
    R, D = X.shape
    return pl.pallas_call(
        functools.partial(_norm_router_kernel, shift=shift, scale=scale),
        grid=(R // tm,),
        in_specs=[pl.BlockSpec((tm, D), lambda i: (i, 0)),
                  pl.BlockSpec((1, D), lambda i: (0, 0)),
                  pl.BlockSpec((1, N_MOD, D), lambda i: (midx(i), 0, 0)),
                  pl.BlockSpec((D, LANES), lambda i: (0, 0)),
                  pl.BlockSpec((N_EXPERTS, 1), lambda i: (0, 0))],
        out_specs=[pl.BlockSpec((tm, D // 2), lambda i: (i, 0)),
                   pl.BlockSpec((8, tm), lambda i: (0, i)),
                   pl.BlockSpec((8, tm), lambda i: (0, i))],
        out_shape=[jax.ShapeDtypeStruct((R, D // 2), U32),
                   jax.ShapeDtypeStruct((8, R), I32),
                   jax.ShapeDtypeStruct((8, R), F32)],
        compiler_params=_cp(("arbitrary",)),
        name="norm_router",
    )(X, g.reshape(1, D), mods, rw_pad, rb)


def _final_norm_kernel(x_ref, g_ref, o_ref):
    o_ref[...] = _norm_rows(x_ref[...], g_ref[...])


def _final_norm(X, g, n_rows, tm):
    D = X.shape[1]
    return pl.pallas_call(
        _final_norm_kernel,
        grid=(n_rows // tm,),
        in_specs=[pl.BlockSpec((tm, D), lambda i: (i, 0)),
                  pl.BlockSpec((1, D), lambda i: (0, 0))],
        out_specs=pl.BlockSpec((tm, D), lambda i: (i, 0)),
        out_shape=jax.ShapeDtypeStruct((n_rows, D), F32),
        compiler_params=_cp(("arbitrary",)),
        name="final_norm",
    )(X, g.reshape(1, D))


MXU_COLS = 256


def _mm_cast(w_ref, wbf_ref):
    @pl.when(pl.program_id(1) == 0)
    def _():
        wbf_ref[...] = w_ref[...].astype(BF16)


def _mm_sub_blocks(a_ref, wbf_ref):
    tn = wbf_ref.shape[1]
    width = min(tn, MXU_COLS)
    for c0 in range(0, tn, width):
        yield c0, _dot(a_ref[...], wbf_ref[:, c0:c0 + width])


def _mm_bias_kernel(a_ref, w_ref, b_ref, o_ref, wbf_ref):
    _mm_cast(w_ref, wbf_ref)
    for c0, acc in _mm_sub_blocks(a_ref, wbf_ref):
        cols = slice(c0, c0 + acc.shape[1])
        o_ref[:, cols] = (acc + b_ref[:, cols]).astype(o_ref.dtype)


def _mm_res_kernel(a_ref, w_ref, r_ref, mod_ref, o_ref, wbf_ref, *, midx):
    _mm_cast(w_ref, wbf_ref)
    for c0, acc in _mm_sub_blocks(a_ref, wbf_ref):
        cols = slice(c0, c0 + acc.shape[1])
        o_ref[:, cols] = r_ref[:, cols] + mod_ref[0, midx:midx + 1, cols] * acc


def _mm_qkv_kernel(a_ref, w_ref, tab_ref, o_ref, wbf_ref, *, n_rope_tiles, n_q_tiles, q_scale, quarter):
    _mm_cast(w_ref, wbf_ref)
    j = pl.program_id(0)

    def store(fn):
        for c0, acc in _mm_sub_blocks(a_ref, wbf_ref):
            for b in range(acc.shape[1] // LANES):
                o_ref[c0 // LANES + b] = fn(acc[:, b * LANES:(b + 1) * LANES]).astype(BF16)

    if n_rope_tiles:
        @pl.when(j < n_rope_tiles)
        def _():
            c, s1, s2 = tab_ref[0, 0], tab_ref[0, 1], tab_ref[0, 2]
            store(lambda x: x * c + pltpu.roll(x, LANES - quarter, 1) * s1 + pltpu.roll(x, quarter, 1) * s2)

        @pl.when(j >= n_rope_tiles)
        def _():
            store(lambda x: x)
    else:
        @pl.when(j < n_q_tiles)
        def _():
            store(lambda x: x * q_scale)

        @pl.when(j >= n_q_tiles)
        def _():
            store(lambda x: x)


def _mm_bias(a, w3, layer, bias, out_dtype, tn_pref=2048):
    M, K = a.shape
    N = w3.shape[2]
    tn = _tile(N, tn_pref)
    return pl.pallas_call(
        _mm_bias_kernel,
        grid=(N // tn, 1),
        in_specs=[pl.BlockSpec((M, K), lambda j, i: (0, 0)),
                  pl.BlockSpec((None, K, tn), lambda j, i: (layer, 0, j), pipeline_mode=pl.Buffered(1)),
                  pl.BlockSpec((1, tn), lambda j, i: (0, j))],
        out_specs=pl.BlockSpec((M, tn), lambda j, i: (0, j)),
        out_shape=jax.ShapeDtypeStruct((M, N), out_dtype),
        scratch_shapes=[pltpu.VMEM((K, tn), BF16)],
        compiler_params=_cp(("arbitrary", "arbitrary")),
        name="mm_bias",
    )(a, w3, bias)


def _mm_res(a, w3, layer, resid, mods, mod_k, tm, midx, tn_pref=1024):
    M, K = a.shape
    N = w3.shape[2]
    tn = _tile(N, tn_pref)
    return pl.pallas_call(
        functools.partial(_mm_res_kernel, midx=mod_k),
        grid=(N // tn, M // tm),
        in_specs=[pl.BlockSpec((tm, K), lambda j, i: (i, 0)),
                  pl.BlockSpec((None, K, tn), lambda j, i: (layer, 0, j), pipeline_mode=pl.Buffered(1)),
                  pl.BlockSpec((tm, tn), lambda j, i: (i, j)),
                  pl.BlockSpec((1, N_MOD, tn), lambda j, i: (midx(i), 0, j))],
        out_specs=pl.BlockSpec((tm, tn), lambda j, i: (i, j)),
        out_shape=jax.ShapeDtypeStruct((M, N), F32),
        scratch_shapes=[pltpu.VMEM((K, tn), BF16)],
        input_output_aliases={2: 0},
        compiler_params=_cp(("arbitrary", "arbitrary")),
        name="mm_res",
    )(a, w3, resid, mods)


def _mm_qkv(a, w3, layer, tabs, tm, tn, n_q_tiles, n_rope_tiles, q_scale, quarter):
    M, K = a.shape
    N = w3.shape[2]
    nblk = tn // LANES
    if tabs is None:
        tabs = jnp.zeros((1, 3, 8, LANES), F32)
        tab_spec = pl.BlockSpec((1, 3, 8, LANES), lambda j, i: (0, 0, 0, 0))
    else:
        tab_spec = pl.BlockSpec((1, 3, tm, LANES), lambda j, i: (jnp.minimum(j // n_q_tiles, 1), 0, i, 0))
    return pl.pallas_call(
        functools.partial(_mm_qkv_kernel, n_rope_tiles=n_rope_tiles, n_q_tiles=n_q_tiles,
                          q_scale=q_scale, quarter=quarter),
        grid=(N // tn, M // tm),
        in_specs=[pl.BlockSpec((tm, K), lambda j, i: (i, 0)),
                  pl.BlockSpec((None, K, tn), lambda j, i: (layer, 0, j), pipeline_mode=pl.Buffered(1)),
                  tab_spec],
        out_specs=pl.BlockSpec((nblk, tm, LANES), lambda j, i: (j, i, 0)),
        out_shape=jax.ShapeDtypeStruct((N // LANES, M, LANES), BF16),
        scratch_shapes=[pltpu.VMEM((K, tn), BF16)],
        compiler_params=_cp(("arbitrary", "arbitrary")),
        name="mm_qkv",
    )(a, w3, tabs)


def _rope_tables(n_lat, n_batch, n_ctx_rows, head_dim, q_scale):
    half, quarter = head_dim // 2, head_dim // 4
    pos = np.arange(n_lat)
    row, col = (pos // GRID_W).astype(np.float32), (pos % GRID_W).astype(np.float32)
    lane = np.arange(LANES) % head_dim
    inv = jnp.asarray(ROPE_THETA, F32) ** (-jnp.arange(quarter, dtype=F32) / quarter)
    p = jnp.where(jnp.asarray(lane // half == 0)[None, :], jnp.asarray(row)[:, None], jnp.asarray(col)[:, None])
    ang = p * inv[np.asarray((lane % half) % quarter)][None, :]
    first = jnp.asarray((lane % half) < quarter)[None, :]
    cos, sin = jnp.cos(ang), jnp.sin(ang)
    t = jnp.stack([cos, jnp.where(first, -sin, 0.0), jnp.where(first, 0.0, sin)])
    t = jnp.tile(t, (1, n_batch, 1))
    ident = jnp.stack([jnp.ones((n_ctx_rows, LANES), F32), jnp.zeros((n_ctx_rows, LANES), F32),
                       jnp.zeros((n_ctx_rows, LANES), F32)])
    t = jnp.concatenate([t, ident], axis=1)
    return jnp.stack([t * q_scale, t])


def _lane_fold(x, op, acc):
    for j in range(x.shape[1] // LANES):
        acc = op(acc, x[:, j * LANES:(j + 1) * LANES])
    return acc


def _da_kernel(lam_ref, q_ref, kc_ref, vc_ref, *rest, has_lat, out_scale, kchunk):
    if has_lat:
        kl_ref, vl_ref, g_ref, o_ref, s0_scr, s1_scr = rest
    else:
        g_ref, _, o_ref, s0_scr, s1_scr = rest
    s_scr = (s0_scr, s1_scr)
    lam = lam_ref[0]
    tq = q_ref.shape[1]
    C = kc_ref.shape[1]
    chunks = [(kc_ref, vc_ref, 0, C, 0)]
    if has_lat:
        chunks += [(kl_ref, vl_ref, j * kchunk, kchunk, C + j * kchunk) for j in range(kl_ref.shape[1] // kchunk)]

    def score_chunk(m, chunk, mrun):
        k_ref, _, off, n, col = chunk
        s = _dot_t(q_ref[m], k_ref[m, off:off + n, :])
        s_scr[m][:, col:col + n] = s
        return _lane_fold(s, jnp.maximum, mrun)

    def prob_chunk(m, chunk, mx, state):
        _, v_ref, off, n, col = chunk
        lrun, o = state
        p = jnp.exp2(s_scr[m][:, col:col + n] - mx)
        v = jnp.concatenate([v_ref[0, off:off + n, :], v_ref[1, off:off + n, :]], axis=1)
        return _lane_fold(p, jnp.add, lrun), o + _dot(p.astype(BF16), v)

    def row_max(mrun):
        return jnp.max(mrun, axis=-1, keepdims=True)

    m_init = jnp.full((tq, LANES), -jnp.inf, F32)
    st_init = (jnp.zeros((tq, LANES), F32), jnp.zeros((tq, 2 * LANES), F32))
    mrun0 = m_init
    for ch in chunks:
        mrun0 = score_chunk(0, ch, mrun0)
    mx0 = row_max(mrun0)
    mrun1, st0 = m_init, st_init
    for ch in chunks:
        mrun1 = score_chunk(1, ch, mrun1)
        st0 = prob_chunk(0, ch, mx0, st0)
    mx1 = row_max(mrun1)
    st1 = st_init
    for ch in chunks:
        st1 = prob_chunk(1, ch, mx1, st1)
    l0 = jnp.sum(st0[0], axis=-1, keepdims=True)
    l1 = jnp.sum(st1[0], axis=-1, keepdims=True)
    o = st0[1] * (1.0 / l0) - st1[1] * (lam / l1)
    o_ref[...] = (_norm_rows(o, g_ref[...]) * out_scale).astype(BF16)


def _da_attention(qkv, lam, subln_g, out_scale, n_batch, S, C, H, tq):
    R = qkv.shape[1]
    n_lat = n_batch * S
    d2 = 2 * DA_HEAD_DIM
    g = subln_g.reshape(1, d2)
    smem = pl.BlockSpec(memory_space=pltpu.SMEM)
    tql = _tile(S, tq, 8)
    kchunk = _tile(S, 512)
    out = pl.pallas_call(
        functools.partial(_da_kernel, has_lat=True, out_scale=out_scale, kchunk=kchunk),
        grid=(n_batch, H, S // tql),
        in_specs=[smem,
                  pl.BlockSpec((2, tql, LANES), lambda b, h, i: (h, b * (S // tql) + i, 0)),
                  pl.BlockSpec((2, C, LANES), lambda b, h, i: (H + h, n_lat // C + b, 0)),
                  pl.BlockSpec((2, C, LANES), lambda b, h, i: (2 * H + h, n_lat // C + b, 0)),
                  pl.BlockSpec((2, S, LANES), lambda b, h, i: (H + h, b, 0)),
                  pl.BlockSpec((2, S, LANES), lambda b, h, i: (2 * H + h, b, 0)),
                  pl.BlockSpec((1, d2), lambda b, h, i: (0, 0))],
        out_specs=pl.BlockSpec((tql, d2), lambda b, h, i: (b * (S // tql) + i, h)),
        out_shape=jax.ShapeDtypeStruct((R, H * d2), BF16),
        scratch_shapes=[pltpu.VMEM((tql, C + S), F32), pltpu.VMEM((tql, C + S), F32)],
        compiler_params=_cp(("arbitrary", "arbitrary", "arbitrary")),
        name="da_attn_lat",
    )(lam, qkv, qkv, qkv, qkv, qkv, g)
    tqc = _tile(C, tq, 8)
    return pl.pallas_call(
        functools.partial(_da_kernel, has_lat=False, out_scale=out_scale, kchunk=kchunk),
        grid=(n_batch, H, C // tqc),
        in_specs=[smem,
                  pl.BlockSpec((2, tqc, LANES), lambda b, h, i: (h, (n_lat + b * C) // tqc + i, 0)),
                  pl.BlockSpec((2, C, LANES), lambda b, h, i: (H + h, n_lat // C + b, 0)),
                  pl.BlockSpec((2, C, LANES), lambda b, h, i: (2 * H + h, n_lat // C + b, 0)),
                  pl.BlockSpec((1, d2), lambda b, h, i: (0, 0)),
                  pl.BlockSpec(memory_space=pl.ANY)],
        out_specs=pl.BlockSpec((tqc, d2), lambda b, h, i: ((n_lat + b * C) // tqc + i, h)),
        out_shape=jax.ShapeDtypeStruct((R, H * d2), BF16),
        scratch_shapes=[pltpu.VMEM((tqc, C), F32), pltpu.VMEM((tqc, C), F32)],
        input_output_aliases={5: 0},
        compiler_params=_cp(("arbitrary", "arbitrary", "arbitrary")),
        name="da_attn_ctx",
    )(lam, qkv, qkv, qkv, g, out)


def _na_geometry(rows):
    kh, rq = min(NA_KH, rows), NA_ROWS_PER_TILE
    kr = kh + rq
    assert rows % rq == 0 and rows >= kr and rows // rq >= 3
    return kh, rq, kr


def _na_bias(rpb, rows):
    kh, rq, kr = _na_geometry(rows)
    W, kw = GRID_W, NA_KW
    n_tiles = rows // rq
    H = rpb.shape[0]

    def row_geometry(i):
        r = i * rq + np.arange(rq)
        start = int(np.clip(i * rq - kh // 2, 0, rows - kr))
        rs = np.clip(r - kh // 2, 0, rows - kh)
        return [(int(a), int(d)) for a, d in zip(rs - start, rs - r + (NA_KH - 1))]

    pats = [row_geometry(0), row_geometry(1), row_geometry(n_tiles - 1)]
    for i in range(1, n_tiles - 1):
        assert row_geometry(i) == pats[1]

    L = 2 * W - 1
    u = jnp.pad(rpb.astype(F32), ((0, 0), (0, 0), (0, L - rpb.shape[2])))
    t = jnp.tile(u, (1, 1, W))[:, :, :W * (L - 1)].reshape(H, rpb.shape[1], W, L - 1)[..., kw - 1:kw - 1 + W]
    c = np.arange(W)
    cs = np.clip(c - kw // 2, 0, W - kw)
    vcol = (c[None, :] >= cs[:, None]) & (c[None, :] < cs[:, None] + kw)
    t = jnp.where(vcol[None, None], t, NEG_BIG)
    t = jnp.transpose(t, (0, 2, 1, 3))

    def neg(n):
        return jnp.full((H, W, n, W), NEG_BIG, F32)

    out = []
    for pat in pats:
        per_row = [jnp.concatenate([neg(a), t[:, :, d:d + kh, :], neg(kr - a - kh)], axis=2) for a, d in pat]
        out.append(jnp.stack(per_row, axis=1).reshape(H, rq * W, kr * W))
    return jnp.stack(out)


def _na_kernel(q_ref, kc_ref, vc_ref, *rest, has_loc, rows):
    if has_loc:
        k_ref, v_ref, bias_ref, o_ref = rest
    else:
        _, o_ref = rest
    q = q_ref[0]
    sc = _dot_t(q, kc_ref[0])
    mx = jnp.max(sc, axis=-1, keepdims=True)
    if has_loc:
        kh, rq, kr = _na_geometry(rows)
        start = jnp.clip(pl.program_id(2) * rq - kh // 2, 0, rows - kr) * GRID_W
        start = pl.multiple_of(start, GRID_W)
        sl = _dot_t(q, k_ref[0, pl.ds(start, kr * GRID_W), :]) + bias_ref[0, 0]
        mx = jnp.maximum(mx, jnp.max(sl, axis=-1, keepdims=True))
    pc = jnp.exp(sc - mx)
    l = jnp.sum(pc, axis=-1, keepdims=True)
    o = _dot(pc.astype(BF16), vc_ref[0])
    if has_loc:
        pl_ = jnp.exp(sl - mx)
        l = l + jnp.sum(pl_, axis=-1, keepdims=True)
        o = o + _dot(pl_.astype(BF16), v_ref[0, pl.ds(start, kr * GRID_W), :])
    o_ref[...] = (o * (1.0 / l)).astype(BF16)


def _na_attention(qkv, bias, n_batch, S, C, H):
    R = qkv.shape[1]
    n_lat = n_batch * S
    rows = S // GRID_W
    kh, rq, kr = _na_geometry(rows)
    tq, n_loc, n_tiles = rq * GRID_W, kr * GRID_W, rows // rq
    d = NA_HEAD_DIM

    def pat(i):
        return jnp.where(i == 0, 0, jnp.where(i == n_tiles - 1, 2, 1))

    out = pl.pallas_call(
        functools.partial(_na_kernel, has_loc=True, rows=rows),
        grid=(n_batch, H, n_tiles),
        in_specs=[pl.BlockSpec((1, tq, LANES), lambda b, h, i: (h, b * n_tiles + i, 0)),
                  pl.BlockSpec((1, C, LANES), lambda b, h, i: (H + h, n_lat // C + b, 0)),
                  pl.BlockSpec((1, C, LANES), lambda b, h, i: (2 * H + h, n_lat // C + b, 0)),
                  pl.BlockSpec((1, S, LANES), lambda b, h, i: (H + h, b, 0)),
                  pl.BlockSpec((1, S, LANES), lambda b, h, i: (2 * H + h, b, 0)),
                  pl.BlockSpec((1, 1, tq, n_loc), lambda b, h, i: (pat(i), h, 0, 0))],
        out_specs=pl.BlockSpec((tq, d), lambda b, h, i: (b * n_tiles + i, h)),
        out_shape=jax.ShapeDtypeStruct((R, H * d), BF16),
        compiler_params=_cp(("arbitrary", "arbitrary", "arbitrary")),
        name="na_attn_lat",
    )(qkv, qkv, qkv, qkv, qkv, bias)
    return pl.pallas_call(
        functools.partial(_na_kernel, has_loc=False, rows=rows),
        grid=(n_batch, H, 1),
        in_specs=[pl.BlockSpec((1, C, LANES), lambda b, h, i: (h, n_lat // C + b, 0)),
                  pl.BlockSpec((1, C, LANES), lambda b, h, i: (H + h, n_lat // C + b, 0)),
                  pl.BlockSpec((1, C, LANES), lambda b, h, i: (2 * H + h, n_lat // C + b, 0)),
                  pl.BlockSpec(memory_space=pl.ANY)],
        out_specs=pl.BlockSpec((C, d), lambda b, h, i: (n_lat // C + b, h)),
        out_shape=jax.ShapeDtypeStruct((R, H * d), BF16),
        input_output_aliases={3: 0},
        compiler_params=_cp(("arbitrary", "arbitrary", "arbitrary")),
        name="na_attn_ctx",
    )(qkv, qkv, qkv, out)


def _sw_kernel(sink_ref, q_ref, kc_ref, vc_ref, *rest, has_loc, tq, band, S, G):
    if has_loc:
        k_ref, v_ref, o_ref = rest
    else:
        _, o_ref = rest
    hd = SW_HEAD_DIM
    per_blk = LANES // hd
    g2n = G // per_blk
    C = kc_ref.shape[1]
    c, i = pl.program_id(1), pl.program_id(2)
    lane = lax.broadcasted_iota(I32, (1, LANES), 1)

    def block_diag(x, par):
        sw = jnp.concatenate([x[:, hd:], x[:, :hd]], axis=1)
        zero = jnp.zeros_like(x)
        top = jnp.where(lane < hd, x if par == 0 else sw, zero)
        bot = jnp.where(lane >= hd, sw if par == 0 else x, zero)
        return jnp.concatenate([top, bot], axis=0)

    if has_loc:
        bstart = pl.multiple_of(jnp.clip(i * tq - SW_WINDOW, 0, S - band), LANES)
        kb = k_ref[0, pl.ds(bstart, band), :]
        vb = v_ref[0, pl.ds(bstart, band), :]
        qpos = i * tq + lax.broadcasted_iota(I32, (tq, 1), 0)
        kpos = bstart + lax.broadcasted_iota(I32, (1, band), 1)
        valid = jnp.abs(kpos - qpos) <= SW_WINDOW

    for par in range(2):
        kc_bd, vc_bd = block_diag(kc_ref[0], par), block_diag(vc_ref[0], par)
        if has_loc:
            kl_bd, vl_bd = block_diag(kb, par), block_diag(vb, par)
        for g2 in range(g2n):
            blk = par * g2n + g2
            q = q_ref[blk]
            sc2 = _dot_t(q, kc_bd)
            if has_loc:
                sl2 = _dot_t(q, kl_bd)
            pcs, pls, rls = [], [], []
            for hh in range(per_blk):
                sink = sink_ref[(2 * c + par) * G + g2 * per_blk + hh]
                sc = sc2[:, hh * C:(hh + 1) * C]
                mx = jnp.maximum(jnp.max(sc, axis=-1, keepdims=True), sink)
                if has_loc:
                    sl = jnp.where(valid, sl2[:, hh * band:(hh + 1) * band], NEG_BIG)
                    mx = jnp.maximum(mx, jnp.max(sl, axis=-1, keepdims=True))
                pc = jnp.exp(sc - mx)
                l = jnp.sum(pc, axis=-1, keepdims=True) + jnp.exp(sink - mx)
                pcs.append(pc)
                if has_loc:
                    pl_ = jnp.exp(sl - mx)
                    l = l + jnp.sum(pl_, axis=-1, keepdims=True)
                    pls.append(pl_)
                rls.append(1.0 / l)
            o = _dot(jnp.concatenate(pcs, axis=1).astype(BF16), vc_bd)
            if has_loc:
                o = o + _dot(jnp.concatenate(pls, axis=1).astype(BF16), vl_bd)
            o = o * jnp.where(lane < hd, rls[0], rls[1])
            o_ref[:, blk * LANES:(blk + 1) * LANES] = o.astype(BF16)


def _sw_attention(qkv, sink, n_batch, S, C, H, tq):
    R = qkv.shape[1]
    n_lat = n_batch * S
    hd, KV = SW_HEAD_DIM, SW_KV_HEADS
    G = H // KV
    nqb, nkb = H * hd // LANES, KV * hd // LANES
    qpk = 2 * G * hd // LANES
    assert KV % 2 == 0 and LANES // hd == 2 and G % 2 == 0
    tq = _tile(S, tq)
    band = tq + 2 * SW_WINDOW
    assert band <= S and SW_WINDOW % LANES == 0
    smem = pl.BlockSpec(memory_space=pltpu.SMEM)
    out = pl.pallas_call(
        functools.partial(_sw_kernel, has_loc=True, tq=tq, band=band, S=S, G=G),
        grid=(n_batch, KV // 2, S // tq),
        in_specs=[smem,
                  pl.BlockSpec((qpk, tq, LANES), lambda b, c, i: (c, b * (S // tq) + i, 0)),
                  pl.BlockSpec((1, C, LANES), lambda b, c, i: (nqb + c, n_lat // C + b, 0)),
                  pl.BlockSpec((1, C, LANES), lambda b, c, i: (nqb + nkb + c, n_lat // C + b, 0)),
                  pl.BlockSpec((1, S, LANES), lambda b, c, i: (nqb + c, b, 0)),
                  pl.BlockSpec((1, S, LANES), lambda b, c, i: (nqb + nkb + c, b, 0))],
        out_specs=pl.BlockSpec((tq, qpk * LANES), lambda b, c, i: (b * (S // tq) + i, c)),
        out_shape=jax.ShapeDtypeStruct((R, H * hd), BF16),
        compiler_params=_cp(("arbitrary", "arbitrary", "arbitrary")),
        name="sw_attn_lat",
    )(sink, qkv, qkv, qkv, qkv, qkv)
    return pl.pallas_call(
        functools.partial(_sw_kernel, has_loc=False, tq=C, band=0, S=S, G=G),
        grid=(n_batch, KV // 2, 1),
        in_specs=[smem,
                  pl.BlockSpec((qpk, C, LANES), lambda b, c, i: (c, n_lat // C + b, 0)),
                  pl.BlockSpec((1, C, LANES), lambda b, c, i: (nqb + c, n_lat // C + b, 0)),
                  pl.BlockSpec((1, C, LANES), lambda b, c, i: (nqb + nkb + c, n_lat // C + b, 0)),
                  pl.BlockSpec(memory_space=pl.ANY)],
        out_specs=pl.BlockSpec((C, qpk * LANES), lambda b, c, i: (n_lat // C + b, c)),
        out_shape=jax.ShapeDtypeStruct((R, H * hd), BF16),
        input_output_aliases={4: 0},
        compiler_params=_cp(("arbitrary", "arbitrary", "arbitrary")),
        name="sw_attn_ctx",
    )(sink, qkv, qkv, qkv, out)


def _moe_plan(eidx, tme):
    e_flat = eidx[:2].reshape(-1)
    onehot = (e_flat[:, None] == jnp.arange(N_EXPERTS, dtype=I32)[None, :]).astype(I32)
    csum = jnp.cumsum(onehot, axis=0)
    rank = jnp.sum(csum * onehot, axis=1) - 1
    cnt = csum[-1]
    pcnt = (cnt + tme - 1) // tme * tme
    ends = jnp.cumsum(pcnt)
    offs = ends - pcnt
    dest = (offs[e_flat] + rank).astype(I32)
    pad_row = jnp.where(cnt % tme != 0, ends - tme, -1).astype(I32)
    return dest, (pcnt // tme).astype(I32), (offs // tme).astype(I32), pad_row


def _work_list(tiles_e, tile_base, n_chunks, n_items_max):
    items_e = tiles_e * n_chunks
    ends = jnp.cumsum(items_e)
    nw = ends[-1]
    w = jnp.minimum(jnp.arange(n_items_max, dtype=I32), nw - 1)
    e = jnp.sum((w[:, None] >= ends[None, :]).astype(I32), axis=1)
    local = w - (ends[e] - items_e[e])
    te = jnp.maximum(tiles_e[e], 1)
    chunk, r = local // te, local % te
    return (e.astype(I32), chunk.astype(I32), (tile_base[e] + r).astype(I32),
            (r == 0).astype(I32), nw.astype(I32).reshape(1))


def _row_copy(src_hbm, src_row, dst, dst_row, sem):
    return pltpu.make_async_copy(src_hbm.at[pl.ds(src_row, 1)], dst.at[pl.ds(dst_row, 1)], sem)


def _scatter_rows_kernel(dest_ref, pad_ref, hp_ref, o_hbm, zbuf, sem, zsem, *, ts, n_rows, tme):
    def zero_copy(e):
        row = pl.multiple_of(pad_ref[e], 8)
        return pltpu.make_async_copy(zbuf, o_hbm.at[pl.ds(row, tme)], zsem)

    @pl.when(pl.program_id(0) == 0)
    def _():
        zbuf[...] = jnp.zeros_like(zbuf)
        for e in range(N_EXPERTS):
            pl.when(pad_ref[e] >= 0)(lambda e=e: zero_copy(e).start())
        for e in range(N_EXPERTS):
            pl.when(pad_ref[e] >= 0)(lambda e=e: zero_copy(e).wait())

    base = pl.program_id(0) * ts

    def issue(r, carry):
        for k in range(2):
            _row_copy(hp_ref, r, o_hbm, dest_ref[k * n_rows + base + r], sem).start()
        return carry

    def drain(r, carry):
        for k in range(2):
            _row_copy(hp_ref, r, o_hbm, 0, sem).wait()
        return carry

    lax.fori_loop(0, ts, issue, 0)
    lax.fori_loop(0, ts, drain, 0)


def _scatter_rows(hp, dest, pad_row, tme, p_max):
    R, W = hp.shape
    ts = _tile(R, 256, 8)
    return pl.pallas_call(
        functools.partial(_scatter_rows_kernel, ts=ts, n_rows=R, tme=tme),
        grid_spec=pltpu.PrefetchScalarGridSpec(
            num_scalar_prefetch=2, grid=(R // ts,),
            in_specs=[pl.BlockSpec((ts, W), lambda i, d, p: (i, 0))],
            out_specs=pl.BlockSpec(memory_space=pl.ANY),
            scratch_shapes=[pltpu.VMEM((tme, W), hp.dtype), pltpu.SemaphoreType.DMA(()),
                            pltpu.SemaphoreType.DMA(())]),
        out_shape=jax.ShapeDtypeStruct((p_max, W), hp.dtype),
        compiler_params=_cp(("arbitrary",)),
        name="moe_scatter",
    )(dest, pad_row, hp)


def _silu(g):
    return g / (1.0 + jnp.exp(-g))


def _moe_up_kernel(ie, ic, it, ifirst, nw, hp_ref, wg_ref, wu_ref, a_ref, wgb, wub):
    w = pl.program_id(0)

    @pl.when(w < nw[0])
    def _():
        @pl.when(ifirst[w] == 1)
        def _():
            wgb[...] = wg_ref[...].astype(BF16)
            wub[...] = wu_ref[...].astype(BF16)

        hp = hp_ref[...]
        half = hp.shape[1]
        lo = pltpu.bitcast(hp << 16, F32).astype(BF16)
        hi = pltpu.bitcast(hp & jnp.uint32(0xFFFF0000), F32).astype(BF16)
        g = _dot(lo, wgb[:half]) + _dot(hi, wgb[half:])
        u = _dot(lo, wub[:half]) + _dot(hi, wub[half:])
        a_ref[...] = (_silu(g) * u).astype(BF16)


def _moe_down_kernel(ie, ic, it, ifirst, nw, a_ref, wd_ref, y_ref, wdb):
    w = pl.program_id(0)

    @pl.when(w < nw[0])
    def _():
        @pl.when(ifirst[w] == 1)
        def _():
            wdb[...] = wd_ref[...].astype(BF16)

        y_ref[...] = _dot(a_ref[...], wdb[...])


def _combine_kernel(dest_ref, y_hbm, x_ref, mod_ref, gate_ref, o_ref, buf, sem, *, tc, n_rows, mod_k):
    base = pl.program_id(0) * tc

    def issue(r, carry):
        for k in range(2):
            _row_copy(y_hbm, dest_ref[k * n_rows + base + r], buf.at[k], r, sem).start()
        return carry

    def drain(r, carry):
        for k in range(2):
            _row_copy(y_hbm, 0, buf.at[k], r, sem).wait()
        return carry

    lax.fori_loop(0, tc, issue, 0)
    lax.fori_loop(0, tc, drain, 0)
    y = gate_ref[:, 0:1] * buf[0] + gate_ref[:, 1:2] * buf[1]
    o_ref[...] = x_ref[...] + mod_ref[0, mod_k:mod_k + 1, :] * y


def _moe(X, hp, eidx, gw, w_gate, w_up, w_down, layer, mods, mod_k, midx):
    R, D = X.shape
    d_ff = w_gate.shape[3]
    tme = MOE_ROW_TILE
    p_max = -(-(2 * R + N_EXPERTS * (tme - 1)) // tme) * tme
    t_max = p_max // tme
    dest, tiles_e, tile_base, pad_row = _moe_plan(eidx, tme)
    hs = _scatter_rows(hp, dest, pad_row, tme, p_max)

    fc = _tile(d_ff, 384)
    n_f = d_ff // fc
    items = _work_list(tiles_e, tile_base, n_f, n_f * t_max)
    a = pl.pallas_call(
        _moe_up_kernel,
        grid_spec=pltpu.PrefetchScalarGridSpec(
            num_scalar_prefetch=5, grid=(n_f * t_max,),
            in_specs=[pl.BlockSpec((tme, D // 2), lambda w, ie, ic, it, fi, nw: (it[w], 0)),
                      pl.BlockSpec((None, None, D, fc), lambda w, ie, ic, it, fi, nw: (layer, ie[w], 0, ic[w])),
                      pl.BlockSpec((None, None, D, fc), lambda w, ie, ic, it, fi, nw: (layer, ie[w], 0, ic[w]))],
            out_specs=pl.BlockSpec((tme, fc), lambda w, ie, ic, it, fi, nw: (it[w], ic[w])),
            scratch_shapes=[pltpu.VMEM((D, fc), BF16), pltpu.VMEM((D, fc), BF16)]),
        out_shape=jax.ShapeDtypeStruct((p_max, d_ff), BF16),
        compiler_params=_cp(("arbitrary",)),
        name="moe_up",
    )(*items, hs, w_gate, w_up)

    nc = _tile(D, 2048)
    n_n = D // nc
    items = _work_list(tiles_e, tile_base, n_n, n_n * t_max)
    y = pl.pallas_call(
        _moe_down_kernel,
        grid_spec=pltpu.PrefetchScalarGridSpec(
            num_scalar_prefetch=5, grid=(n_n * t_max,),
            in_specs=[pl.BlockSpec((tme, d_ff), lambda w, ie, ic, it, fi, nw: (it[w], 0)),
                      pl.BlockSpec((None, None, d_ff, nc), lambda w, ie, ic, it, fi, nw: (layer, ie[w], 0, ic[w]))],
            out_specs=pl.BlockSpec((tme, nc), lambda w, ie, ic, it, fi, nw: (it[w], ic[w])),
            scratch_shapes=[pltpu.VMEM((d_ff, nc), BF16)]),
        out_shape=jax.ShapeDtypeStruct((p_max, D), F32),
        compiler_params=_cp(("arbitrary",)),
        name="moe_down",
    )(*items, a, w_down)

    tc = _tile(R, 128, 8)
    mod_of_tile = midx(tc)
    return pl.pallas_call(
        functools.partial(_combine_kernel, tc=tc, n_rows=R, mod_k=mod_k),
        grid_spec=pltpu.PrefetchScalarGridSpec(
            num_scalar_prefetch=1, grid=(R // tc,),
            in_specs=[pl.BlockSpec(memory_space=pl.ANY),
                      pl.BlockSpec((tc, D), lambda i, d: (i, 0)),
                      pl.BlockSpec((1, N_MOD, D), lambda i, d: (mod_of_tile(i), 0, 0)),
                      pl.BlockSpec((tc, 2), lambda i, d: (i, 0))],
            out_specs=pl.BlockSpec((tc, D), lambda i, d: (i, 0)),
            scratch_shapes=[pltpu.VMEM((2, tc, D), F32), pltpu.SemaphoreType.DMA(())]),
        out_shape=jax.ShapeDtypeStruct((R, D), F32),
        input_output_aliases={2: 0},
        compiler_params=_cp(("arbitrary",)),
        name="moe_combine",
    )(dest, y, X, mods, gw[:2].T)


def kernel(x, c, ctx, c_ctx, cond_down, mod_w, mod_b, norm_g, final_norm_g, da_w_qkv, da_w_o, da_lambda, da_subln_g, na_w_qkv, na_w_o, na_rpb, sw_w_qkv, sw_w_o, sw_sink, router_w, router_bias, moe_w_gate, moe_w_up, moe_w_down):
    B, S, D = x.shape
    C = ctx.shape[1]
    depth = mod_w.shape[0]
    n_lat, n_ctx = B * S, B * C
    R = n_lat + n_ctx

    def midx(tm):
        assert S % tm == 0 and n_ctx % tm == 0
        return _mod_index_fn(n_lat // tm, S // tm, B)

    X = jnp.concatenate([x.reshape(n_lat, D), ctx.reshape(n_ctx, D)], axis=0)

    cond_rows = 16
    cin = jnp.concatenate([c, c_ctx[None, :], jnp.zeros((cond_rows - B - 1, D), F32)], axis=0)
    cin = jax.nn.silu(cin).astype(BF16)
    cond = _mm_bias(cin, cond_down[None], 0, jnp.zeros((1, cond_down.shape[1]), F32), BF16)

    rw_pad = jnp.zeros((D, LANES), F32).at[:, :N_EXPERTS].set(router_w).astype(BF16)
    rb = router_bias.astype(F32).reshape(N_EXPERTS, 1)

    tm_row = _tile(math.gcd(S, n_ctx), 256, 8)
    tm_mm = _tile(math.gcd(S, n_ctx), 512, 8)
    da_h = da_w_qkv.shape[2] // (6 * DA_HEAD_DIM)
    na_h = na_w_qkv.shape[2] // (3 * NA_HEAD_DIM)
    sw_h = sw_sink.shape[1]
    tabs = {}

    for i in range(depth):
        mods = _mm_bias(cond, mod_w, i, mod_b[i][None, :], F32)[:B + 1].reshape(B + 1, N_MOD, D)
        h = _norm_mod(X, norm_g[i, 0], mods, 0, 1, tm_row, midx(tm_row))
        kind, j = i % N_MIXERS, i // N_MIXERS
        if kind == 0:
            d = DA_HEAD_DIM
            qd = da_h * 2 * d
            tn = _tile(qd, 1024)
            if "da" not in tabs:
                tabs["da"] = _rope_tables(S, B, n_ctx, d, d ** -0.5 * math.log2(math.e))
            qkv = _mm_qkv(h, da_w_qkv, j, tabs["da"], tm_mm, tn, qd // tn, 2 * qd // tn, 1.0, d // 4)
            lam_init = 0.8 - 0.6 * math.exp(-0.3 * i)
            lp = da_lambda[j].astype(F32)
            lam = (jnp.exp(jnp.sum(lp[0] * lp[1])) - jnp.exp(jnp.sum(lp[2] * lp[3])) + lam_init).reshape(1)
            o = _da_attention(qkv, lam, da_subln_g[j], 1.0 - lam_init, B, S, C, da_h, 512)
            w_o = da_w_o
        elif kind == 1:
            d = NA_HEAD_DIM
            tn = _tile(na_h * d, 1024)
            qkv = _mm_qkv(h, na_w_qkv, j, None, tm_mm, tn, na_h * d // tn, 0, d ** -0.5, 0)
            o = _na_attention(qkv, _na_bias(na_rpb[j], S // GRID_W), B, S, C, na_h)
            w_o = na_w_o
        else:
            d = SW_HEAD_DIM
            nq, nkv = sw_h * d, SW_KV_HEADS * d
            tn = _tile(math.gcd(nq, nkv), 512)
            if "sw" not in tabs:
                tabs["sw"] = _rope_tables(S, B, n_ctx, d, d ** -0.5)
            qkv = _mm_qkv(h, sw_w_qkv, j, tabs["sw"], tm_mm, tn, nq // tn, (nq + nkv) // tn, 1.0, d // 4)
            o = _sw_attention(qkv, sw_sink[j].astype(F32), B, S, C, sw_h, 256)
            w_o = sw_w_o
        X = _mm_res(o, w_o, j, X, mods, 2, tm_mm, midx(tm_mm))
        hp, eidx, gw = _norm_router(X, norm_g[i, 1], mods, 3, 4, rw_pad, rb, tm_row, midx(tm_row))
        X = _moe(X, hp, eidx, gw, moe_w_gate, moe_w_up, moe_w_down, i, mods, 5, midx)

    tm_fin = _tile(S, 256, 8)
    return _final_norm(X, final_norm_g, n_lat, tm_fin).reshape(B, S, D)
```

```python
import functools
import math

import numpy as np
import jax
import jax.numpy as jnp
from jax import lax
from jax.experimental import pallas as pl
from jax.experimental.pallas import tpu as pltpu

F32 = jnp.float32
BF16 = jnp.bfloat16
U32 = jnp.uint32
I32 = jnp.int32

GRID_W = 64
N_MIXERS = 3
N_MOD = 6
RMS_EPS = 1e-6
ROPE_THETA = 10000.0
DA_HEAD_DIM = 128
NA_HEAD_DIM = 128
NA_KH = 8
NA_KW = 16
SW_HEAD_DIM = 64
SW_KV_HEADS = 8
SW_WINDOW = 128
N_EXPERTS = 16
N_GROUPS = 4
EXPERTS_PER_GROUP = N_EXPERTS // N_GROUPS

LANES = 128
NEG_BIG = -1e30
NA_ROWS_PER_TILE = 4
NA_HEADS_PER_STEP = 4
DA_STATE_ROWS = 128
LOG2E = math.log2(math.e)
MOE_ROW_TILE = 256
VMEM_MB = 56


def _cp(sem, vmem_mb=VMEM_MB, **kw):
    return pltpu.CompilerParams(dimension_semantics=sem, vmem_limit_bytes=vmem_mb << 20, **kw)


def _tile(n, pref, quantum=LANES):
    if n <= pref:
        return n
    t = (pref // quantum) * quantum
    while t > quantum and n % t:
        t -= quantum
    assert n % t == 0, (n, pref)
    return t


def _dot_t(a, b):
    return lax.dot_general(a, b, (((1,), (1,)), ((), ())), preferred_element_type=F32)


def _dot(a, b):
    return jnp.dot(a, b, preferred_element_type=F32)


def _mod_index_fn(n_lat_tiles, tiles_per_batch, n_batch):
    def f(i):
        return jnp.where(i < n_lat_tiles, i // tiles_per_batch, n_batch)
    return f


def _norm_rows(x, g):
    ms = jnp.mean(x * x, axis=-1, keepdims=True)
    return x * lax.rsqrt(ms + RMS_EPS) * g


def _norm_mod_kernel(x_ref, g_ref, mod_ref, h_ref, *, shift, scale):
    y = _norm_rows(x_ref[...], g_ref[...])
    h = y * (1.0 + mod_ref[0, scale:scale + 1, :]) + mod_ref[0, shift:shift + 1, :]
    h_ref[...] = h.astype(BF16)


def _top2_sum(a, b, c, d):
    hi1, lo1 = jnp.maximum(a, b), jnp.minimum(a, b)
    hi2, lo2 = jnp.maximum(c, d), jnp.minimum(c, d)
    return jnp.maximum(hi1, hi2) + jnp.maximum(jnp.minimum(hi1, hi2), jnp.maximum(lo1, lo2))


def _argmax_first(vals):
    best, idx = vals[0], jnp.zeros(vals[0].shape, I32)
    for k in range(1, len(vals)):
        upd = vals[k] > best
        idx = jnp.where(upd, k, idx)
        best = jnp.where(upd, vals[k], best)
    return idx, best


def _store_token_major(ref, x):
    n, k = x.shape[0], x.shape[1] // LANES
    for j in range(k):
        ref[pl.ds(j, n, stride=k), :] = x[:, j * LANES:(j + 1) * LANES]


def _load_token_major(ref, n):
    k = ref.shape[0] // n
    return [ref[pl.ds(j, n, stride=k), :] for j in range(k)]


def _norm_router_kernel(x_ref, g_ref, mod_ref, rw_ref, rb_ref, hp_ref, eidx_ref, gw_ref, *, shift, scale):
    y = _norm_rows(x_ref[...], g_ref[...])
    h = y * (1.0 + mod_ref[0, scale:scale + 1, :]) + mod_ref[0, shift:shift + 1, :]
    hb = h.astype(BF16)
    half = h.shape[1] // 2
    _store_token_major(hp_ref, _pack_bf16_pair(h[:, :half], h[:, half:]))

    logits = _dot(hb, rw_ref[...])
    lt = logits.T[:N_EXPERTS]
    aff = 1.0 / (1.0 + jnp.exp(-lt))
    sel = aff + rb_ref[...]
    sel_rows = [sel[e:e + 1] for e in range(N_EXPERTS)]
    aff_rows = [aff[e:e + 1] for e in range(N_EXPERTS)]
    epg = EXPERTS_PER_GROUP
    gscore = [_top2_sum(*sel_rows[g * epg:(g + 1) * epg]) for g in range(N_GROUPS)]
    bg, _ = _argmax_first(gscore)

    def pick(rows, k):
        out = rows[k]
        for g in range(1, N_GROUPS):
            out = jnp.where(bg == g, rows[g * epg + k], out)
        return out

    s_in = [pick(sel_rows, k) for k in range(epg)]
    a_in = [pick(aff_rows, k) for k in range(epg)]
    i0, _ = _argmax_first(s_in)
    i1, _ = _argmax_first([jnp.where(i0 == k, -jnp.inf, s_in[k]) for k in range(epg)])

    def take(rows, idx):
        out = rows[0]
        for k in range(1, epg):
            out = jnp.where(idx == k, rows[k], out)
        return out

    w0, w1 = take(a_in, i0), take(a_in, i1)
    tot = w0 + w1
    zf = jnp.zeros((6,) + w0.shape[1:], F32)
    gw_ref[...] = jnp.concatenate([w0 / tot, w1 / tot, zf], axis=0)
    eidx_ref[...] = jnp.concatenate([bg * epg + i0, bg * epg + i1, zf.astype(I32)], axis=0)


def _norm_mod(X, g, mods, shift, scale, tm, midx):
    R, D = X.shape
    return pl.pallas_call(
        functools.partial(_norm_mod_kernel, shift=shift, scale=scale),
        grid=(R // tm,),
        in_specs=[pl.BlockSpec((tm, D), lambda i: (i, 0)),
                  pl.BlockSpec((1, D), lambda i: (0, 0)),
                  pl.BlockSpec((1, N_MOD, D), lambda i: (midx(i), 0, 0))],
        out_specs=pl.BlockSpec((tm, D), lambda i: (i, 0)),
        out_shape=jax.ShapeDtypeStruct((R, D), BF16),
        compiler_params=_cp(("arbitrary",)),
        name="norm_mod",
    )(X, g.reshape(1, D), mods)


def _norm_router(X, g, mods, shift, scale, rw_pad, rb, tm, midx):
    R, D = X.shape
    return pl.pallas_call(
        functools.partial(_norm_router_kernel, shift=shift, scale=scale),
        grid=(R // tm,),
        in_specs=[pl.BlockSpec((tm, D), lambda i: (i, 0)),
                  pl.BlockSpec((1, D), lambda i: (0, 0)),
                  pl.BlockSpec((1, N_MOD, D), lambda i: (midx(i), 0, 0)),
                  pl.BlockSpec((D, LANES), lambda i: (0, 0)),
                  pl.BlockSpec((N_EXPERTS, 1), lambda i: (0, 0))],
        out_specs=[pl.BlockSpec((tm * (D // 2 // LANES), LANES), lambda i: (i, 0)),
                   pl.BlockSpec((8, tm), lambda i: (0, i)),
                   pl.BlockSpec((8, tm), lambda i: (0, i))],
        out_shape=[jax.ShapeDtypeStruct((R * (D // 2 // LANES), LANES), U32),
                   jax.ShapeDtypeStruct((8, R), I32),
                   jax.ShapeDtypeStruct((8, R), F32)],
        compiler_params=_cp(("arbitrary",)),
        name="norm_router",
    )(X, g.reshape(1, D), mods, rw_pad, rb)


def _final_norm_kernel(x_ref, g_ref, o_ref):
    o_ref[...] = _norm_rows(x_ref[...], g_ref[...])


def _final_norm(X, g, n_rows, tm):
    D = X.shape[1]
    return pl.pallas_call(
        _final_norm_kernel,
        grid=(n_rows // tm,),
        in_specs=[pl.BlockSpec((tm, D), lambda i: (i, 0)),
                  pl.BlockSpec((1, D), lambda i: (0, 0))],
        out_specs=pl.BlockSpec((tm, D), lambda i: (i, 0)),
        out_shape=jax.ShapeDtypeStruct((n_rows, D), F32),
        compiler_params=_cp(("arbitrary",)),
        name="final_norm",
    )(X, g.reshape(1, D))


MXU_COLS = 256


def _mm_cast(w_ref, wbf_ref):
    @pl.when(pl.program_id(1) == 0)
    def _():
        wbf_ref[...] = w_ref[...].astype(BF16)


def _mm_sub_blocks(a_ref, wbf_ref):
    tn = wbf_ref.shape[1]
    width = min(tn, MXU_COLS)
    for c0 in range(0, tn, width):
        yield c0, _dot(a_ref[...], wbf_ref[:, c0:c0 + width])


def _mm_bias_kernel(a_ref, w_ref, b_ref, o_ref, wbf_ref):
    _mm_cast(w_ref, wbf_ref)
    for c0, acc in _mm_sub_blocks(a_ref, wbf_ref):
        cols = slice(c0, c0 + acc.shape[1])
        o_ref[:, cols] = (acc + b_ref[:, cols]).astype(o_ref.dtype)


def _mm_res_kernel(a_ref, w_ref, r_ref, mod_ref, o_ref, wbf_ref, *, midx):
    _mm_cast(w_ref, wbf_ref)
    for c0, acc in _mm_sub_blocks(a_ref, wbf_ref):
        cols = slice(c0, c0 + acc.shape[1])
        o_ref[:, cols] = r_ref[:, cols] + mod_ref[0, midx:midx + 1, cols] * acc


def _mm_qkv_kernel(a_ref, w_ref, tab_ref, o_ref, wbf_ref, *, n_rope_tiles, n_q_tiles, q_scale, quarter):
    _mm_cast(w_ref, wbf_ref)
    j = pl.program_id(0)

    def store(fn):
        for c0, acc in _mm_sub_blocks(a_ref, wbf_ref):
            for b in range(acc.shape[1] // LANES):
                o_ref[c0 // LANES + b] = fn(acc[:, b * LANES:(b + 1) * LANES]).astype(BF16)

    if n_rope_tiles:
        @pl.when(j < n_rope_tiles)
        def _():
            c, s1, s2 = tab_ref[0, 0], tab_ref[0, 1], tab_ref[0, 2]
            store(lambda x: x * c + pltpu.roll(x, LANES - quarter, 1) * s1 + pltpu.roll(x, quarter, 1) * s2)

        @pl.when(j >= n_rope_tiles)
        def _():
            store(lambda x: x)
    else:
        @pl.when(j < n_q_tiles)
        def _():
            store(lambda x: x * q_scale)

        @pl.when(j >= n_q_tiles)
        def _():
            store(lambda x: x)


def _mm_bias(a, w3, layer, bias, out_dtype, tn_pref=2048):
    M, K = a.shape
    N = w3.shape[2]
    tn = _tile(N, tn_pref)
    return pl.pallas_call(
        _mm_bias_kernel,
        grid=(N // tn, 1),
        in_specs=[pl.BlockSpec((M, K), lambda j, i: (0, 0)),
                  pl.BlockSpec((None, K, tn), lambda j, i: (layer, 0, j), pipeline_mode=pl.Buffered(1)),
                  pl.BlockSpec((1, tn), lambda j, i: (0, j))],
        out_specs=pl.BlockSpec((M, tn), lambda j, i: (0, j)),
        out_shape=jax.ShapeDtypeStruct((M, N), out_dtype),
        scratch_shapes=[pltpu.VMEM((K, tn), BF16)],
        compiler_params=_cp(("arbitrary", "arbitrary")),
        name="mm_bias",
    )(a, w3, bias)


def _mm_res(a, w3, layer, resid, mods, mod_k, tm, midx, tn_pref=1024):
    M, K = a.shape
    N = w3.shape[2]
    tn = _tile(N, tn_pref)
    return pl.pallas_call(
        functools.partial(_mm_res_kernel, midx=mod_k),
        grid=(N // tn, M // tm),
        in_specs=[pl.BlockSpec((tm, K), lambda j, i: (i, 0)),
                  pl.BlockSpec((None, K, tn), lambda j, i: (layer, 0, j), pipeline_mode=pl.Buffered(1)),
                  pl.BlockSpec((tm, tn), lambda j, i: (i, j)),
                  pl.BlockSpec((1, N_MOD, tn), lambda j, i: (midx(i), 0, j))],
        out_specs=pl.BlockSpec((tm, tn), lambda j, i: (i, j)),
        out_shape=jax.ShapeDtypeStruct((M, N), F32),
        scratch_shapes=[pltpu.VMEM((K, tn), BF16)],
        input_output_aliases={2: 0},
        compiler_params=_cp(("arbitrary", "arbitrary")),
        name="mm_res",
    )(a, w3, resid, mods)


def _mm_qkv(a, w3, layer, tabs, tm, tn, n_q_tiles, n_rope_tiles, q_scale, quarter):
    M, K = a.shape
    N = w3.shape[2]
    nblk = tn // LANES
    if tabs is None:
        tabs = jnp.zeros((1, 3, 8, LANES), F32)
        tab_spec = pl.BlockSpec((1, 3, 8, LANES), lambda j, i: (0, 0, 0, 0))
    else:
        tab_spec = pl.BlockSpec((1, 3, tm, LANES), lambda j, i: (jnp.minimum(j // n_q_tiles, 1), 0, i, 0))
    return pl.pallas_call(
        functools.partial(_mm_qkv_kernel, n_rope_tiles=n_rope_tiles, n_q_tiles=n_q_tiles,
                          q_scale=q_scale, quarter=quarter),
        grid=(N // tn, M // tm),
        in_specs=[pl.BlockSpec((tm, K), lambda j, i: (i, 0)),
                  pl.BlockSpec((None, K, tn), lambda j, i: (layer, 0, j), pipeline_mode=pl.Buffered(1)),
                  tab_spec],
        out_specs=pl.BlockSpec((nblk, tm, LANES), lambda j, i: (j, i, 0)),
        out_shape=jax.ShapeDtypeStruct((N // LANES, M, LANES), BF16),
        scratch_shapes=[pltpu.VMEM((K, tn), BF16)],
        compiler_params=_cp(("arbitrary", "arbitrary")),
        name="mm_qkv",
    )(a, w3, tabs)


def _rope_tables(n_lat, n_batch, n_ctx_rows, head_dim, q_scale):
    half, quarter = head_dim // 2, head_dim // 4
    pos = np.arange(n_lat)
    row, col = (pos // GRID_W).astype(np.float32), (pos % GRID_W).astype(np.float32)
    lane = np.arange(LANES) % head_dim
    inv = jnp.asarray(ROPE_THETA, F32) ** (-jnp.arange(quarter, dtype=F32) / quarter)
    p = jnp.where(jnp.asarray(lane // half == 0)[None, :], jnp.asarray(row)[:, None], jnp.asarray(col)[:, None])
    ang = p * inv[np.asarray((lane % half) % quarter)][None, :]
    first = jnp.asarray((lane % half) < quarter)[None, :]
    cos, sin = jnp.cos(ang), jnp.sin(ang)
    t = jnp.stack([cos, jnp.where(first, -sin, 0.0), jnp.where(first, 0.0, sin)])
    t = jnp.tile(t, (1, n_batch, 1))
    ident = jnp.stack([jnp.ones((n_ctx_rows, LANES), F32), jnp.zeros((n_ctx_rows, LANES), F32),
                       jnp.zeros((n_ctx_rows, LANES), F32)])
    t = jnp.concatenate([t, ident], axis=1)
    return jnp.stack([t * q_scale, t])


def _lane_fold(x, op, acc):
    for j in range(x.shape[1] // LANES):
        acc = op(acc, x[:, j * LANES:(j + 1) * LANES])
    return acc


def _da_kernel(lam_ref, q_ref, kc_ref, vc_ref, *rest, has_lat, out_scale, kchunk):
    if has_lat:
        kl_ref, vl_ref, g_ref, o_ref, s0_scr, s1_scr = rest
    else:
        g_ref, _, o_ref, s0_scr, s1_scr = rest
    s_scr = (s0_scr, s1_scr)
    lam = lam_ref[0]
    tq = q_ref.shape[1]
    C = kc_ref.shape[1]
    chunks = [(kc_ref, vc_ref, 0, C, 0)]
    if has_lat:
        chunks += [(kl_ref, vl_ref, j * kchunk, kchunk, C + j * kchunk) for j in range(kl_ref.shape[1] // kchunk)]

    rsub = math.gcd(tq, DA_STATE_ROWS)
    row_blocks = [slice(r, r + rsub) for r in range(0, tq, rsub)]

    def score_chunk(m, chunk, mrun):
        k_ref, _, off, n, col = chunk
        s = _dot_t(q_ref[m], k_ref[m, off:off + n, :])
        s_scr[m][:, col:col + n] = s
        return [_lane_fold(s[rows], jnp.maximum, mr) for rows, mr in zip(row_blocks, mrun)]

    def prob_chunk(m, chunk, mx, state):
        _, v_ref, off, n, col = chunk
        lrun, o = state
        p, lnew = [], []
        for rows, mxr, lr in zip(row_blocks, mx, lrun):
            pr = jnp.exp2(s_scr[m][rows, col:col + n] - mxr)
            lnew.append(_lane_fold(pr, jnp.add, lr))
            p.append(pr.astype(BF16))
        v = jnp.concatenate([v_ref[0, off:off + n, :], v_ref[1, off:off + n, :]], axis=1)
        return lnew, o + _dot(jnp.concatenate(p, axis=0), v)

    def row_max(mrun):
        return [jnp.max(mr, axis=-1, keepdims=True) for mr in mrun]

    def row_sum(lrun):
        return jnp.concatenate([jnp.sum(lr, axis=-1, keepdims=True) for lr in lrun], axis=0)

    m_init = [jnp.full((rsub, LANES), -jnp.inf, F32) for _ in row_blocks]
    st_init = ([jnp.zeros((rsub, LANES), F32) for _ in row_blocks], jnp.zeros((tq, 2 * LANES), F32))
    mrun0 = m_init
    for ch in chunks:
        mrun0 = score_chunk(0, ch, mrun0)
    mx0 = row_max(mrun0)
    mrun1, st0 = m_init, st_init
    for ch in chunks:
        mrun1 = score_chunk(1, ch, mrun1)
        st0 = prob_chunk(0, ch, mx0, st0)
    mx1 = row_max(mrun1)
    st1 = st_init
    for ch in chunks:
        st1 = prob_chunk(1, ch, mx1, st1)
    o = st0[1] * (1.0 / row_sum(st0[0])) - st1[1] * (lam / row_sum(st1[0]))
    o_ref[...] = (_norm_rows(o, g_ref[...]) * out_scale).astype(BF16)


def _da_attention(qkv, lam, subln_g, out_scale, n_batch, S, C, H, tq):
    R = qkv.shape[1]
    n_lat = n_batch * S
    d2 = 2 * DA_HEAD_DIM
    g = subln_g.reshape(1, d2)
    smem = pl.BlockSpec(memory_space=pltpu.SMEM)
    tql = _tile(S, tq, 8)
    kchunk = _tile(S, 512)
    out = pl.pallas_call(
        functools.partial(_da_kernel, has_lat=True, out_scale=out_scale, kchunk=kchunk),
        grid=(n_batch, H, S // tql),
        in_specs=[smem,
                  pl.BlockSpec((2, tql, LANES), lambda b, h, i: (h, b * (S // tql) + i, 0)),
                  pl.BlockSpec((2, C, LANES), lambda b, h, i: (H + h, n_lat // C + b, 0)),
                  pl.BlockSpec((2, C, LANES), lambda b, h, i: (2 * H + h, n_lat // C + b, 0)),
                  pl.BlockSpec((2, S, LANES), lambda b, h, i: (H + h, b, 0)),
                  pl.BlockSpec((2, S, LANES), lambda b, h, i: (2 * H + h, b, 0)),
                  pl.BlockSpec((1, d2), lambda b, h, i: (0, 0))],
        out_specs=pl.BlockSpec((tql, d2), lambda b, h, i: (b * (S // tql) + i, h)),
        out_shape=jax.ShapeDtypeStruct((R, H * d2), BF16),
        scratch_shapes=[pltpu.VMEM((tql, C + S), F32), pltpu.VMEM((tql, C + S), F32)],
        compiler_params=_cp(("arbitrary", "arbitrary", "arbitrary")),
        name="da_attn_lat",
    )(lam, qkv, qkv, qkv, qkv, qkv, g)
    tqc = _tile(C, tq, 8)
    return pl.pallas_call(
        functools.partial(_da_kernel, has_lat=False, out_scale=out_scale, kchunk=kchunk),
        grid=(n_batch, H, C // tqc),
        in_specs=[smem,
                  pl.BlockSpec((2, tqc, LANES), lambda b, h, i: (h, (n_lat + b * C) // tqc + i, 0)),
                  pl.BlockSpec((2, C, LANES), lambda b, h, i: (H + h, n_lat // C + b, 0)),
                  pl.BlockSpec((2, C, LANES), lambda b, h, i: (2 * H + h, n_lat // C + b, 0)),
                  pl.BlockSpec((1, d2), lambda b, h, i: (0, 0)),
                  pl.BlockSpec(memory_space=pl.ANY)],
        out_specs=pl.BlockSpec((tqc, d2), lambda b, h, i: ((n_lat + b * C) // tqc + i, h)),
        out_shape=jax.ShapeDtypeStruct((R, H * d2), BF16),
        scratch_shapes=[pltpu.VMEM((tqc, C), F32), pltpu.VMEM((tqc, C), F32)],
        input_output_aliases={5: 0},
        compiler_params=_cp(("arbitrary", "arbitrary", "arbitrary")),
        name="da_attn_ctx",
    )(lam, qkv, qkv, qkv, g, out)


def _na_geometry(rows):
    kh, rq = min(NA_KH, rows), NA_ROWS_PER_TILE
    kr = kh + rq
    assert rows % rq == 0 and rows >= kr and rows // rq >= 3
    return kh, rq, kr


def _na_bias(rpb, rows):
    kh, rq, kr = _na_geometry(rows)
    W, kw = GRID_W, NA_KW
    n_tiles = rows // rq
    H = rpb.shape[0]

    def row_geometry(i):
        r = i * rq + np.arange(rq)
        start = int(np.clip(i * rq - kh // 2, 0, rows - kr))
        rs = np.clip(r - kh // 2, 0, rows - kh)
        return [(int(a), int(d)) for a, d in zip(rs - start, rs - r + (NA_KH - 1))]

    pats = [row_geometry(0), row_geometry(1), row_geometry(n_tiles - 1)]
    for i in range(1, n_tiles - 1):
        assert row_geometry(i) == pats[1]

    L = 2 * W - 1
    u = jnp.pad(rpb.astype(F32), ((0, 0), (0, 0), (0, L - rpb.shape[2])))
    t = jnp.tile(u, (1, 1, W))[:, :, :W * (L - 1)].reshape(H, rpb.shape[1], W, L - 1)[..., kw - 1:kw - 1 + W]
    c = np.arange(W)
    cs = np.clip(c - kw // 2, 0, W - kw)
    vcol = (c[None, :] >= cs[:, None]) & (c[None, :] < cs[:, None] + kw)
    t = jnp.where(vcol[None, None], t * LOG2E, NEG_BIG)
    t = jnp.transpose(t, (0, 2, 1, 3))

    def neg(n):
        return jnp.full((H, W, n, W), NEG_BIG, F32)

    out = []
    for pat in pats:
        per_row = [jnp.concatenate([neg(a), t[:, :, d:d + kh, :], neg(kr - a - kh)], axis=2) for a, d in pat]
        out.append(jnp.stack(per_row, axis=1).reshape(H, rq * W, kr * W))
    return jnp.stack(out)


def _na_kernel(q_ref, kc_ref, vc_ref, *rest, has_loc, rows):
    if has_loc:
        k_ref, v_ref, bias_ref, o_ref = rest
        kh, rq, kr = _na_geometry(rows)
        start = jnp.clip(pl.program_id(2) * rq - kh // 2, 0, rows - kr) * GRID_W
        start = pl.multiple_of(start, GRID_W)
    else:
        _, o_ref = rest
    for hb in range(q_ref.shape[0]):
        q = q_ref[hb]
        sc = _dot_t(q, kc_ref[hb])
        mx = jnp.max(sc, axis=-1, keepdims=True)
        if has_loc:
            sl = _dot_t(q, k_ref[hb, pl.ds(start, kr * GRID_W), :]) + bias_ref[0, hb]
            mx = jnp.maximum(mx, jnp.max(sl, axis=-1, keepdims=True))
        pc = jnp.exp2(sc - mx)
        l = jnp.sum(pc, axis=-1, keepdims=True)
        o = _dot(pc.astype(BF16), vc_ref[hb])
        if has_loc:
            pl_ = jnp.exp2(sl - mx)
            l = l + jnp.sum(pl_, axis=-1, keepdims=True)
            o = o + _dot(pl_.astype(BF16), v_ref[hb, pl.ds(start, kr * GRID_W), :])
        o_ref[:, hb * LANES:(hb + 1) * LANES] = (o * (1.0 / l)).astype(BF16)


def _na_attention(qkv, bias, n_batch, S, C, H):
    R = qkv.shape[1]
    n_lat = n_batch * S
    rows = S // GRID_W
    kh, rq, kr = _na_geometry(rows)
    tq, n_loc, n_tiles = rq * GRID_W, kr * GRID_W, rows // rq
    d = NA_HEAD_DIM

    def pat(i):
        return jnp.where(i == 0, 0, jnp.where(i == n_tiles - 1, 2, 1))

    hb = math.gcd(H, NA_HEADS_PER_STEP)
    nhb = H // hb
    out = pl.pallas_call(
        functools.partial(_na_kernel, has_loc=True, rows=rows),
        grid=(n_batch, nhb, n_tiles),
        in_specs=[pl.BlockSpec((hb, tq, LANES), lambda b, h, i: (h, b * n_tiles + i, 0)),
                  pl.BlockSpec((hb, C, LANES), lambda b, h, i: (nhb + h, n_lat // C + b, 0)),
                  pl.BlockSpec((hb, C, LANES), lambda b, h, i: (2 * nhb + h, n_lat // C + b, 0)),
                  pl.BlockSpec((hb, S, LANES), lambda b, h, i: (nhb + h, b, 0)),
                  pl.BlockSpec((hb, S, LANES), lambda b, h, i: (2 * nhb + h, b, 0)),
                  pl.BlockSpec((1, hb, tq, n_loc), lambda b, h, i: (pat(i), h, 0, 0))],
        out_specs=pl.BlockSpec((tq, hb * d), lambda b, h, i: (b * n_tiles + i, h)),
        out_shape=jax.ShapeDtypeStruct((R, H * d), BF16),
        compiler_params=_cp(("arbitrary", "arbitrary", "arbitrary")),
        name="na_attn_lat",
    )(qkv, qkv, qkv, qkv, qkv, bias)
    return pl.pallas_call(
        functools.partial(_na_kernel, has_loc=False, rows=rows),
        grid=(n_batch, nhb, 1),
        in_specs=[pl.BlockSpec((hb, C, LANES), lambda b, h, i: (h, n_lat // C + b, 0)),
                  pl.BlockSpec((hb, C, LANES), lambda b, h, i: (nhb + h, n_lat // C + b, 0)),
                  pl.BlockSpec((hb, C, LANES), lambda b, h, i: (2 * nhb + h, n_lat // C + b, 0)),
                  pl.BlockSpec(memory_space=pl.ANY)],
        out_specs=pl.BlockSpec((C, hb * d), lambda b, h, i: (n_lat // C + b, h)),
        out_shape=jax.ShapeDtypeStruct((R, H * d), BF16),
        input_output_aliases={3: 0},
        compiler_params=_cp(("arbitrary", "arbitrary", "arbitrary")),
        name="na_attn_ctx",
    )(qkv, qkv, qkv, out)


def _sw_kernel(sink_ref, q_ref, kc_ref, vc_ref, *rest, has_loc, tq, band, S, G):
    if has_loc:
        k_ref, v_ref, o_ref = rest
    else:
        _, o_ref = rest
    hd = SW_HEAD_DIM
    per_blk = LANES // hd
    g2n = G // per_blk
    C = kc_ref.shape[1]
    c, i = pl.program_id(1), pl.program_id(2)
    lane = lax.broadcasted_iota(I32, (1, LANES), 1)

    def block_diag(x, par):
        sw = jnp.concatenate([x[:, hd:], x[:, :hd]], axis=1)
        zero = jnp.zeros_like(x)
        top = jnp.where(lane < hd, x if par == 0 else sw, zero)
        bot = jnp.where(lane >= hd, sw if par == 0 else x, zero)
        return jnp.concatenate([top, bot], axis=0)

    if has_loc:
        bstart = pl.multiple_of(jnp.clip(i * tq - SW_WINDOW, 0, S - band), LANES)
        kb = k_ref[0, pl.ds(bstart, band), :]
        vb = v_ref[0, pl.ds(bstart, band), :]
        qpos = i * tq + lax.broadcasted_iota(I32, (tq, 1), 0)
        kpos = bstart + lax.broadcasted_iota(I32, (1, band), 1)
        valid = jnp.abs(kpos - qpos) <= SW_WINDOW

    for par in range(2):
        kc_bd, vc_bd = block_diag(kc_ref[0], par), block_diag(vc_ref[0], par)
        if has_loc:
            kl_bd, vl_bd = block_diag(kb, par), block_diag(vb, par)
        for g2 in range(g2n):
            blk = par * g2n + g2
            q = q_ref[blk]
            sc2 = _dot_t(q, kc_bd)
            if has_loc:
                sl2 = _dot_t(q, kl_bd)
            pcs, pls, rls = [], [], []
            for hh in range(per_blk):
                sink = sink_ref[(2 * c + par) * G + g2 * per_blk + hh] * LOG2E
                sc = sc2[:, hh * C:(hh + 1) * C]
                mx = jnp.maximum(jnp.max(sc, axis=-1, keepdims=True), sink)
                if has_loc:
                    sl = jnp.where(valid, sl2[:, hh * band:(hh + 1) * band], NEG_BIG)
                    mx = jnp.maximum(mx, jnp.max(sl, axis=-1, keepdims=True))
                pc = jnp.exp2(sc - mx)
                l = jnp.sum(pc, axis=-1, keepdims=True) + jnp.exp2(sink - mx)
                pcs.append(pc)
                if has_loc:
                    pl_ = jnp.exp2(sl - mx)
                    l = l + jnp.sum(pl_, axis=-1, keepdims=True)
                    pls.append(pl_)
                rls.append(1.0 / l)
            o = _dot(jnp.concatenate(pcs, axis=1).astype(BF16), vc_bd)
            if has_loc:
                o = o + _dot(jnp.concatenate(pls, axis=1).astype(BF16), vl_bd)
            o = o * jnp.where(lane < hd, rls[0], rls[1])
            o_ref[:, blk * LANES:(blk + 1) * LANES] = o.astype(BF16)


def _sw_attention(qkv, sink, n_batch, S, C, H, tq):
    R = qkv.shape[1]
    n_lat = n_batch * S
    hd, KV = SW_HEAD_DIM, SW_KV_HEADS
    G = H // KV
    nqb, nkb = H * hd // LANES, KV * hd // LANES
    qpk = 2 * G * hd // LANES
    assert KV % 2 == 0 and LANES // hd == 2 and G % 2 == 0
    tq = _tile(S, tq)
    band = tq + 2 * SW_WINDOW
    assert band <= S and SW_WINDOW % LANES == 0
    smem = pl.BlockSpec(memory_space=pltpu.SMEM)
    out = pl.pallas_call(
        functools.partial(_sw_kernel, has_loc=True, tq=tq, band=band, S=S, G=G),
        grid=(n_batch, KV // 2, S // tq),
        in_specs=[smem,
                  pl.BlockSpec((qpk, tq, LANES), lambda b, c, i: (c, b * (S // tq) + i, 0)),
                  pl.BlockSpec((1, C, LANES), lambda b, c, i: (nqb + c, n_lat // C + b, 0)),
                  pl.BlockSpec((1, C, LANES), lambda b, c, i: (nqb + nkb + c, n_lat // C + b, 0)),
                  pl.BlockSpec((1, S, LANES), lambda b, c, i: (nqb + c, b, 0)),
                  pl.BlockSpec((1, S, LANES), lambda b, c, i: (nqb + nkb + c, b, 0))],
        out_specs=pl.BlockSpec((tq, qpk * LANES), lambda b, c, i: (b * (S // tq) + i, c)),
        out_shape=jax.ShapeDtypeStruct((R, H * hd), BF16),
        compiler_params=_cp(("arbitrary", "arbitrary", "arbitrary")),
        name="sw_attn_lat",
    )(sink, qkv, qkv, qkv, qkv, qkv)
    return pl.pallas_call(
        functools.partial(_sw_kernel, has_loc=False, tq=C, band=0, S=S, G=G),
        grid=(n_batch, KV // 2, 1),
        in_specs=[smem,
                  pl.BlockSpec((qpk, C, LANES), lambda b, c, i: (c, n_lat // C + b, 0)),
                  pl.BlockSpec((1, C, LANES), lambda b, c, i: (nqb + c, n_lat // C + b, 0)),
                  pl.BlockSpec((1, C, LANES), lambda b, c, i: (nqb + nkb + c, n_lat // C + b, 0)),
                  pl.BlockSpec(memory_space=pl.ANY)],
        out_specs=pl.BlockSpec((C, qpk * LANES), lambda b, c, i: (n_lat // C + b, c)),
        out_shape=jax.ShapeDtypeStruct((R, H * hd), BF16),
        input_output_aliases={4: 0},
        compiler_params=_cp(("arbitrary", "arbitrary", "arbitrary")),
        name="sw_attn_ctx",
    )(sink, qkv, qkv, qkv, out)


def _moe_plan(eidx, tme):
    e_flat = eidx[:2].reshape(-1)
    onehot = (e_flat[:, None] == jnp.arange(N_EXPERTS, dtype=I32)[None, :]).astype(I32)
    csum = jnp.cumsum(onehot, axis=0)
    rank = jnp.sum(csum * onehot, axis=1) - 1
    cnt = csum[-1]
    pcnt = (cnt + tme - 1) // tme * tme
    ends = jnp.cumsum(pcnt)
    offs = ends - pcnt
    dest = (offs[e_flat] + rank).astype(I32)
    pad_row = jnp.where(cnt % tme != 0, ends - tme, -1).astype(I32)
    return dest, (pcnt // tme).astype(I32), (offs // tme).astype(I32), pad_row


def _work_list(tiles_e, tile_base, n_chunks, n_items_max):
    items_e = tiles_e * n_chunks
    ends = jnp.cumsum(items_e)
    nw = ends[-1]
    w = jnp.minimum(jnp.arange(n_items_max, dtype=I32), nw - 1)
    e = jnp.sum((w[:, None] >= ends[None, :]).astype(I32), axis=1)
    local = w - (ends[e] - items_e[e])
    te = jnp.maximum(tiles_e[e], 1)
    chunk, r = local // te, local % te
    return (e.astype(I32), chunk.astype(I32), (tile_base[e] + r).astype(I32),
            (r == 0).astype(I32), nw.astype(I32).reshape(1))


def _token_copy(src, src_tok, dst, dst_tok, k, sem):
    def first_row(tok):
        return tok * k if isinstance(tok, int) else pl.multiple_of(tok * k, k)

    return pltpu.make_async_copy(src.at[pl.ds(first_row(src_tok), k)], dst.at[pl.ds(first_row(dst_tok), k)], sem)


def _scatter_rows_kernel(dest_ref, pad_ref, hp_ref, o_hbm, zbuf, sem, zsem, *, ts, n_rows, tme, k):
    def zero_copy(e):
        row = pl.multiple_of(pad_ref[e] * k, 8)
        return pltpu.make_async_copy(zbuf, o_hbm.at[pl.ds(row, tme * k)], zsem)

    @pl.when(pl.program_id(0) == 0)
    def _():
        zbuf[...] = jnp.zeros_like(zbuf)
        for e in range(N_EXPERTS):
            pl.when(pad_ref[e] >= 0)(lambda e=e: zero_copy(e).start())
        for e in range(N_EXPERTS):
            pl.when(pad_ref[e] >= 0)(lambda e=e: zero_copy(e).wait())

    base = pl.program_id(0) * ts

    def issue(r, carry):
        for slot in range(2):
            _token_copy(hp_ref, r, o_hbm, dest_ref[slot * n_rows + base + r], k, sem).start()
        return carry

    def drain(r, carry):
        for slot in range(2):
            _token_copy(hp_ref, r, o_hbm, 0, k, sem).wait()
        return carry

    lax.fori_loop(0, ts, issue, 0)
    lax.fori_loop(0, ts, drain, 0)


def _scatter_rows(hp, dest, pad_row, n_rows, tme, p_max):
    k = hp.shape[0] // n_rows
    ts = _tile(n_rows, 256, 8)
    return pl.pallas_call(
        functools.partial(_scatter_rows_kernel, ts=ts, n_rows=n_rows, tme=tme, k=k),
        grid_spec=pltpu.PrefetchScalarGridSpec(
            num_scalar_prefetch=2, grid=(n_rows // ts,),
            in_specs=[pl.BlockSpec((ts * k, LANES), lambda i, d, p: (i, 0))],
            out_specs=pl.BlockSpec(memory_space=pl.ANY),
            scratch_shapes=[pltpu.VMEM((tme * k, LANES), hp.dtype), pltpu.SemaphoreType.DMA(()),
                            pltpu.SemaphoreType.DMA(())]),
        out_shape=jax.ShapeDtypeStruct((p_max * k, LANES), hp.dtype),
        compiler_params=_cp(("arbitrary",)),
        name="moe_scatter",
    )(dest, pad_row, hp)


def _silu(g):
    return g / (1.0 + jnp.exp(-g))


def _moe_up_kernel(ie, ic, it, ifirst, nw, hp_ref, wg_ref, wu_ref, a_ref, wgb, wub):
    w = pl.program_id(0)

    @pl.when(w < nw[0])
    def _():
        @pl.when(ifirst[w] == 1)
        def _():
            wgb[...] = wg_ref[...].astype(BF16)
            wub[...] = wu_ref[...].astype(BF16)

        hp = jnp.concatenate(_load_token_major(hp_ref, a_ref.shape[0]), axis=1)
        half = hp.shape[1]
        lo, hi = (v.astype(BF16) for v in _unpack_bf16_pair(hp))
        g = _dot(lo, wgb[:half]) + _dot(hi, wgb[half:])
        u = _dot(lo, wub[:half]) + _dot(hi, wub[half:])
        a_ref[...] = (_silu(g) * u).astype(BF16)


def _moe_down_kernel(ie, ic, it, ifirst, nw, a_ref, wd_ref, y_ref, wdb):
    w = pl.program_id(0)

    @pl.when(w < nw[0])
    def _():
        @pl.when(ifirst[w] == 1)
        def _():
            wdb[...] = wd_ref[...].astype(BF16)

        half = y_ref.shape[1]
        width = min(half, MXU_COLS)
        a = a_ref[...]
        for c0 in range(0, half, width):
            lo = _dot(a, wdb[:, c0:c0 + width])
            hi = _dot(a, wdb[:, half + c0:half + c0 + width])
            y_ref[:, c0:c0 + width] = _pack_bf16_pair(lo, hi)


def _pack_bf16_pair(lo, hi):
    lo_bits = pltpu.bitcast(lo.astype(BF16).astype(F32), U32)
    hi_bits = pltpu.bitcast(hi.astype(BF16).astype(F32), U32)
    return (lo_bits >> 16) | (hi_bits & jnp.uint32(0xFFFF0000))


def _unpack_bf16_pair(p):
    return pltpu.bitcast(p << 16, F32), pltpu.bitcast(p & jnp.uint32(0xFFFF0000), F32)


def _row_copy(src_hbm, src_row, dst, dst_row, sem):
    return pltpu.make_async_copy(src_hbm.at[pl.ds(src_row, 1)], dst.at[pl.ds(dst_row, 1)], sem)


def _combine_kernel(dest_ref, y_hbm, x_ref, mod_ref, gate_ref, o_ref, buf, sem, *, tc, n_rows, mod_k):
    base = pl.program_id(0) * tc

    def issue(r, carry):
        for slot in range(2):
            _row_copy(y_hbm, dest_ref[slot * n_rows + base + r], buf.at[slot], r, sem).start()
        return carry

    def drain(r, carry):
        for slot in range(2):
            _row_copy(y_hbm, 0, buf.at[slot], r, sem).wait()
        return carry

    lax.fori_loop(0, tc, issue, 0)
    lax.fori_loop(0, tc, drain, 0)
    half = buf.shape[2]
    g0, g1 = gate_ref[:, 0:1], gate_ref[:, 1:2]
    width = min(half, 4 * LANES)
    for c0 in range(0, half, width):
        cols = slice(c0, c0 + width)
        y0_lo, y0_hi = _unpack_bf16_pair(buf[0, :, cols])
        y1_lo, y1_hi = _unpack_bf16_pair(buf[1, :, cols])
        for off, y in ((0, g0 * y0_lo + g1 * y1_lo), (half, g0 * y0_hi + g1 * y1_hi)):
            out_cols = slice(off + c0, off + c0 + width)
            o_ref[:, out_cols] = x_ref[:, out_cols] + mod_ref[0, mod_k:mod_k + 1, out_cols] * y


def _moe(X, hp, eidx, gw, w_gate, w_up, w_down, layer, mods, mod_k, midx):
    R, D = X.shape
    d_ff = w_gate.shape[3]
    tme = MOE_ROW_TILE
    p_max = -(-(2 * R + N_EXPERTS * (tme - 1)) // tme) * tme
    t_max = p_max // tme
    dest, tiles_e, tile_base, pad_row = _moe_plan(eidx, tme)
    hs = _scatter_rows(hp, dest, pad_row, R, tme, p_max)
    kp = D // 2 // LANES

    fc = _tile(d_ff, MXU_COLS)
    n_f = d_ff // fc
    items = _work_list(tiles_e, tile_base, n_f, n_f * t_max)
    a = pl.pallas_call(
        _moe_up_kernel,
        grid_spec=pltpu.PrefetchScalarGridSpec(
            num_scalar_prefetch=5, grid=(n_f * t_max,),
            in_specs=[pl.BlockSpec((tme * kp, LANES), lambda w, ie, ic, it, fi, nw: (it[w], 0)),
                      pl.BlockSpec((None, None, D, fc), lambda w, ie, ic, it, fi, nw: (layer, ie[w], 0, ic[w])),
                      pl.BlockSpec((None, None, D, fc), lambda w, ie, ic, it, fi, nw: (layer, ie[w], 0, ic[w]))],
            out_specs=pl.BlockSpec((tme, fc), lambda w, ie, ic, it, fi, nw: (it[w], ic[w])),
            scratch_shapes=[pltpu.VMEM((D, fc), BF16), pltpu.VMEM((D, fc), BF16)]),
        out_shape=jax.ShapeDtypeStruct((p_max, d_ff), BF16),
        compiler_params=_cp(("arbitrary",)),
        name="moe_up",
    )(*items, hs, w_gate, w_up)

    items = _work_list(tiles_e, tile_base, 1, t_max)
    y = pl.pallas_call(
        _moe_down_kernel,
        grid_spec=pltpu.PrefetchScalarGridSpec(
            num_scalar_prefetch=5, grid=(t_max,),
            in_specs=[pl.BlockSpec((tme, d_ff), lambda w, ie, ic, it, fi, nw: (it[w], 0)),
                      pl.BlockSpec((None, None, d_ff, D), lambda w, ie, ic, it, fi, nw: (layer, ie[w], 0, 0))],
            out_specs=pl.BlockSpec((tme, D // 2), lambda w, ie, ic, it, fi, nw: (it[w], 0)),
            scratch_shapes=[pltpu.VMEM((d_ff, D), BF16)]),
        out_shape=jax.ShapeDtypeStruct((p_max, D // 2), U32),
        compiler_params=_cp(("arbitrary",)),
        name="moe_down",
    )(*items, a, w_down)

    tc = _tile(R, 128, 8)
    mod_of_tile = midx(tc)
    return pl.pallas_call(
        functools.partial(_combine_kernel, tc=tc, n_rows=R, mod_k=mod_k),
        grid_spec=pltpu.PrefetchScalarGridSpec(
            num_scalar_prefetch=1, grid=(R // tc,),
            in_specs=[pl.BlockSpec(memory_space=pl.ANY),
                      pl.BlockSpec((tc, D), lambda i, d: (i, 0)),
                      pl.BlockSpec((1, N_MOD, D), lambda i, d: (mod_of_tile(i), 0, 0)),
                      pl.BlockSpec((tc, 2), lambda i, d: (i, 0))],
            out_specs=pl.BlockSpec((tc, D), lambda i, d: (i, 0)),
            scratch_shapes=[pltpu.VMEM((2, tc, D // 2), U32), pltpu.SemaphoreType.DMA(())]),
        out_shape=jax.ShapeDtypeStruct((R, D), F32),
        input_output_aliases={2: 0},
        compiler_params=_cp(("arbitrary",)),
        name="moe_combine",
    )(dest, y, X, mods, gw[:2].T)


def kernel(x, c, ctx, c_ctx, cond_down, mod_w, mod_b, norm_g, final_norm_g, da_w_qkv, da_w_o, da_lambda, da_subln_g, na_w_qkv, na_w_o, na_rpb, sw_w_qkv, sw_w_o, sw_sink, router_w, router_bias, moe_w_gate, moe_w_up, moe_w_down):
    B, S, D = x.shape
    C = ctx.shape[1]
    depth = mod_w.shape[0]
    n_lat, n_ctx = B * S, B * C
    R = n_lat + n_ctx

    def midx(tm):
        assert S % tm == 0 and n_ctx % tm == 0
        return _mod_index_fn(n_lat // tm, S // tm, B)

    X = jnp.concatenate([x.reshape(n_lat, D), ctx.reshape(n_ctx, D)], axis=0)

    cond_rows = 16
    cin = jnp.concatenate([c, c_ctx[None, :], jnp.zeros((cond_rows - B - 1, D), F32)], axis=0)
    cin = jax.nn.silu(cin).astype(BF16)
    cond = _mm_bias(cin, cond_down[None], 0, jnp.zeros((1, cond_down.shape[1]), F32), BF16)

    rw_pad = jnp.zeros((D, LANES), F32).at[:, :N_EXPERTS].set(router_w).astype(BF16)
    rb = router_bias.astype(F32).reshape(N_EXPERTS, 1)

    tm_row = _tile(math.gcd(S, n_ctx), 256, 8)
    tm_mm = _tile(math.gcd(S, n_ctx), 512, 8)
    da_h = da_w_qkv.shape[2] // (6 * DA_HEAD_DIM)
    na_h = na_w_qkv.shape[2] // (3 * NA_HEAD_DIM)
    sw_h = sw_sink.shape[1]
    tabs = {}

    for i in range(depth):
        mods = _mm_bias(cond, mod_w, i, mod_b[i][None, :], F32)[:B + 1].reshape(B + 1, N_MOD, D)
        h = _norm_mod(X, norm_g[i, 0], mods, 0, 1, tm_row, midx(tm_row))
        kind, j = i % N_MIXERS, i // N_MIXERS
        if kind == 0:
            d = DA_HEAD_DIM
            qd = da_h * 2 * d
            tn = _tile(qd, 1024)
            if "da" not in tabs:
                tabs["da"] = _rope_tables(S, B, n_ctx, d, d ** -0.5 * LOG2E)
            qkv = _mm_qkv(h, da_w_qkv, j, tabs["da"], tm_mm, tn, qd // tn, 2 * qd // tn, 1.0, d // 4)
            lam_init = 0.8 - 0.6 * math.exp(-0.3 * i)
            lp = da_lambda[j].astype(F32)
            lam = (jnp.exp(jnp.sum(lp[0] * lp[1])) - jnp.exp(jnp.sum(lp[2] * lp[3])) + lam_init).reshape(1)
            o = _da_attention(qkv, lam, da_subln_g[j], 1.0 - lam_init, B, S, C, da_h, 512)
            w_o = da_w_o
        elif kind == 1:
            d = NA_HEAD_DIM
            tn = _tile(na_h * d, 1024)
            qkv = _mm_qkv(h, na_w_qkv, j, None, tm_mm, tn, na_h * d // tn, 0, d ** -0.5 * LOG2E, 0)
            o = _na_attention(qkv, _na_bias(na_rpb[j], S // GRID_W), B, S, C, na_h)
            w_o = na_w_o
        else:
            d = SW_HEAD_DIM
            nq, nkv = sw_h * d, SW_KV_HEADS * d
            tn = _tile(math.gcd(nq, nkv), 512)
            if "sw" not in tabs:
                tabs["sw"] = _rope_tables(S, B, n_ctx, d, d ** -0.5 * LOG2E)
            qkv = _mm_qkv(h, sw_w_qkv, j, tabs["sw"], tm_mm, tn, nq // tn, (nq + nkv) // tn, 1.0, d // 4)
            o = _sw_attention(qkv, sw_sink[j].astype(F32), B, S, C, sw_h, 256)
            w_o = sw_w_o
        X = _mm_res(o, w_o, j, X, mods, 2, tm_mm, midx(tm_mm))
        hp, eidx, gw = _norm_router(X, norm_g[i, 1], mods, 3, 4, rw_pad, rb, tm_row, midx(tm_row))
        X = _moe(X, hp, eidx, gw, moe_w_gate, moe_w_up, moe_w_down, i, mods, 5, midx)

    tm_fin = _tile(S, 256, 8)
    return _final_norm(X, final_norm_g, n_lat, tm_fin).reshape(B, S, D)
```

```python
import functools
import math

import numpy as np
import jax
import jax.numpy as jnp
from jax import lax
from jax.experimental import pallas as pl
from jax.experimental.pallas import tpu as pltpu

F32 = jnp.float32
BF16 = jnp.bfloat16
U32 = jnp.uint32
I32 = jnp.int32

GRID_W = 64
N_MIXERS = 3
N_MOD = 6
RMS_EPS = 1e-6
ROPE_THETA = 10000.0
DA_HEAD_DIM = 128
NA_HEAD_DIM = 128
NA_KH = 8
NA_KW = 16
SW_HEAD_DIM = 64
SW_KV_HEADS = 8
SW_WINDOW = 128
N_EXPERTS = 16
N_GROUPS = 4
EXPERTS_PER_GROUP = N_EXPERTS // N_GROUPS

LANES = 128
NEG_BIG = -1e30
NA_ROWS_PER_TILE = 4
NA_HEADS_PER_STEP = 4
DA_STATE_ROWS = 128
LOG2E = math.log2(math.e)
MOE_ROW_TILE = 256
VMEM_MB = 56


def _cp(sem, vmem_mb=VMEM_MB, **kw):
    return pltpu.CompilerParams(dimension_semantics=sem, vmem_limit_bytes=vmem_mb << 20, **kw)


def _tile(n, pref, quantum=LANES):
    if n <= pref:
        return n
    t = (pref // quantum) * quantum
    while t > quantum and n % t:
        t -= quantum
    assert n % t == 0, (n, pref)
    return t


def _dot_t(a, b):
    return lax.dot_general(a, b, (((1,), (1,)), ((), ())), preferred_element_type=F32)


def _dot(a, b):
    return jnp.dot(a, b, preferred_element_type=F32)


def _mod_index_fn(n_lat_tiles, tiles_per_batch, n_batch):
    def f(i):
        return jnp.where(i < n_lat_tiles, i // tiles_per_batch, n_batch)
    return f


def _norm_rows(x, g):
    ms = jnp.mean(x * x, axis=-1, keepdims=True)
    return x * lax.rsqrt(ms + RMS_EPS) * g


def _norm_mod_kernel(x_ref, g_ref, mod_ref, h_ref, *, shift, scale):
    y = _norm_rows(x_ref[...], g_ref[...])
    h = y * (1.0 + mod_ref[0, scale:scale + 1, :]) + mod_ref[0, shift:shift + 1, :]
    h_ref[...] = h.astype(BF16)


def _top2_sum(a, b, c, d):
    hi1, lo1 = jnp.maximum(a, b), jnp.minimum(a, b)
    hi2, lo2 = jnp.maximum(c, d), jnp.minimum(c, d)
    return jnp.maximum(hi1, hi2) + jnp.maximum(jnp.minimum(hi1, hi2), jnp.maximum(lo1, lo2))


def _argmax_first(vals):
    best, idx = vals[0], jnp.zeros(vals[0].shape, I32)
    for k in range(1, len(vals)):
        upd = vals[k] > best
        idx = jnp.where(upd, k, idx)
        best = jnp.where(upd, vals[k], best)
    return idx, best


def _store_token_major(ref, x):
    n, k = x.shape[0], x.shape[1] // LANES
    for j in range(k):
        ref[pl.ds(j, n, stride=k), :] = x[:, j * LANES:(j + 1) * LANES]


def _load_token_major(ref, n):
    k = ref.shape[0] // n
    return [ref[pl.ds(j, n, stride=k), :] for j in range(k)]


def _norm_router_kernel(x_ref, g_ref, mod_ref, rw_ref, rb_ref, hp_ref, eidx_ref, gw_ref, *, shift, scale):
    y = _norm_rows(x_ref[...], g_ref[...])
    h = y * (1.0 + mod_ref[0, scale:scale + 1, :]) + mod_ref[0, shift:shift + 1, :]
    hb = h.astype(BF16)
    half = h.shape[1] // 2
    _store_token_major(hp_ref, _pack_bf16_pair(h[:, :half], h[:, half:]))

    logits = _dot(hb, rw_ref[...])
    lt = logits.T[:N_EXPERTS]
    aff = 1.0 / (1.0 + jnp.exp(-lt))
    sel = aff + rb_ref[...]
    sel_rows = [sel[e:e + 1] for e in range(N_EXPERTS)]
    aff_rows = [aff[e:e + 1] for e in range(N_EXPERTS)]
    epg = EXPERTS_PER_GROUP
    gscore = [_top2_sum(*sel_rows[g * epg:(g + 1) * epg]) for g in range(N_GROUPS)]
    bg, _ = _argmax_first(gscore)

    def pick(rows, k):
        out = rows[k]
        for g in range(1, N_GROUPS):
            out = jnp.where(bg == g, rows[g * epg + k], out)
        return out

    s_in = [pick(sel_rows, k) for k in range(epg)]
    a_in = [pick(aff_rows, k) for k in range(epg)]
    i0, _ = _argmax_first(s_in)
    i1, _ = _argmax_first([jnp.where(i0 == k, -jnp.inf, s_in[k]) for k in range(epg)])

    def take(rows, idx):
        out = rows[0]
        for k in range(1, epg):
            out = jnp.where(idx == k, rows[k], out)
        return out

    w0, w1 = take(a_in, i0), take(a_in, i1)
    tot = w0 + w1
    zf = jnp.zeros((6,) + w0.shape[1:], F32)
    gw_ref[...] = jnp.concatenate([w0 / tot, w1 / tot, zf], axis=0)
    eidx_ref[...] = jnp.concatenate([bg * epg + i0, bg * epg + i1, zf.astype(I32)], axis=0)


def _norm_mod(X, g, mods, shift, scale, tm, midx):
    R, D = X.shape
    return pl.pallas_call(
        functools.partial(_norm_mod_kernel, shift=shift, scale=scale),
        grid=(R // tm,),
        in_specs=[pl.BlockSpec((tm, D), lambda i: (i, 0)),
                  pl.BlockSpec((1, D), lambda i: (0, 0)),
                  pl.BlockSpec((1, N_MOD, D), lambda i: (midx(i), 0, 0))],
        out_specs=pl.BlockSpec((tm, D), lambda i: (i, 0)),
        out_shape=jax.ShapeDtypeStruct((R, D), BF16),
        compiler_params=_cp(("arbitrary",)),
        name="norm_mod",
    )(X, g.reshape(1, D), mods)


def _norm_router(X, g, mods, shift, scale, rw_pad, rb, tm, midx):
    R, D = X.shape
    return pl.pallas_call(
        functools.partial(_norm_router_kernel, shift=shift, scale=scale),
        grid=(R // tm,),
        in_specs=[pl.BlockSpec((tm, D), lambda i: (i, 0)),
                  pl.BlockSpec((1, D), lambda i: (0, 0)),
                  pl.BlockSpec((1, N_MOD, D), lambda i: (midx(i), 0, 0)),
                  pl.BlockSpec((D, LANES), lambda i: (0, 0)),
                  pl.BlockSpec((N_EXPERTS, 1), lambda i: (0, 0))],
        out_specs=[pl.BlockSpec((tm * (D // 2 // LANES), LANES), lambda i: (i, 0)),
                   pl.BlockSpec((8, tm), lambda i: (0, i)),
                   pl.BlockSpec((8, tm), lambda i: (0, i))],
        out_shape=[jax.ShapeDtypeStruct((R * (D // 2 // LANES), LANES), U32),
                   jax.ShapeDtypeStruct((8, R), I32),
                   jax.ShapeDtypeStruct((8, R), F32)],
        compiler_params=_cp(("arbitrary",)),
        name="norm_router",
    )(X, g.reshape(1, D), mods, rw_pad, rb)


def _final_norm_kernel(x_ref, g_ref, o_ref):
    o_ref[...] = _norm_rows(x_ref[...], g_ref[...])


def _final_norm(X, g, n_rows, tm):
    D = X.shape[1]
    return pl.pallas_call(
        _final_norm_kernel,
        grid=(n_rows // tm,),
        in_specs=[pl.BlockSpec((tm, D), lambda i: (i, 0)),
                  pl.BlockSpec((1, D), lambda i: (0, 0))],
        out_specs=pl.BlockSpec((tm, D), lambda i: (i, 0)),
        out_shape=jax.ShapeDtypeStruct((n_rows, D), F32),
        compiler_params=_cp(("arbitrary",)),
        name="final_norm",
    )(X, g.reshape(1, D))


MXU_COLS = 256


def _mm_cast(w_ref, wbf_ref):
    @pl.when(pl.program_id(1) == 0)
    def _():
        wbf_ref[...] = w_ref[...].astype(BF16)


def _mm_sub_blocks(a_ref, wbf_ref):
    tn = wbf_ref.shape[1]
    width = min(tn, MXU_COLS)
    for c0 in range(0, tn, width):
        yield c0, _dot(a_ref[...], wbf_ref[:, c0:c0 + width])


def _mm_bias_kernel(a_ref, w_ref, b_ref, o_ref, wbf_ref):
    _mm_cast(w_ref, wbf_ref)
    for c0, acc in _mm_sub_blocks(a_ref, wbf_ref):
        cols = slice(c0, c0 + acc.shape[1])
        o_ref[:, cols] = (acc + b_ref[:, cols]).astype(o_ref.dtype)


def _mm_res_kernel(a_ref, w_ref, r_ref, mod_ref, o_ref, wbf_ref, *, midx):
    _mm_cast(w_ref, wbf_ref)
    for c0, acc in _mm_sub_blocks(a_ref, wbf_ref):
        cols = slice(c0, c0 + acc.shape[1])
        o_ref[:, cols] = r_ref[:, cols] + mod_ref[0, midx:midx + 1, cols] * acc


def _mm_qkv_kernel(a_ref, w_ref, tab_ref, o_ref, wbf_ref, *, n_rope_tiles, n_q_tiles, q_scale, quarter):
    _mm_cast(w_ref, wbf_ref)
    j = pl.program_id(0)

    def store(fn):
        for c0, acc in _mm_sub_blocks(a_ref, wbf_ref):
            for b in range(acc.shape[1] // LANES):
                o_ref[c0 // LANES + b] = fn(acc[:, b * LANES:(b + 1) * LANES]).astype(BF16)

    if n_rope_tiles:
        @pl.when(j < n_rope_tiles)
        def _():
            c, s1, s2 = tab_ref[0, 0], tab_ref[0, 1], tab_ref[0, 2]
            store(lambda x: x * c + pltpu.roll(x, LANES - quarter, 1) * s1 + pltpu.roll(x, quarter, 1) * s2)

        @pl.when(j >= n_rope_tiles)
        def _():
            store(lambda x: x)
    else:
        @pl.when(j < n_q_tiles)
        def _():
            store(lambda x: x * q_scale)

        @pl.when(j >= n_q_tiles)
        def _():
            store(lambda x: x)


def _mm_bias(a, w3, layer, bias, out_dtype, tn_pref=2048):
    M, K = a.shape
    N = w3.shape[2]
    tn = _tile(N, tn_pref)
    return pl.pallas_call(
        _mm_bias_kernel,
        grid=(N // tn, 1),
        in_specs=[pl.BlockSpec((M, K), lambda j, i: (0, 0)),
                  pl.BlockSpec((None, K, tn), lambda j, i: (layer, 0, j), pipeline_mode=pl.Buffered(1)),
                  pl.BlockSpec((1, tn), lambda j, i: (0, j))],
        out_specs=pl.BlockSpec((M, tn), lambda j, i: (0, j)),
        out_shape=jax.ShapeDtypeStruct((M, N), out_dtype),
        scratch_shapes=[pltpu.VMEM((K, tn), BF16)],
        compiler_params=_cp(("arbitrary", "arbitrary")),
        name="mm_bias",
    )(a, w3, bias)


def _mm_res(a, w3, layer, resid, mods, mod_k, tm, midx, tn_pref=1024):
    M, K = a.shape
    N = w3.shape[2]
    tn = _tile(N, tn_pref)
    return pl.pallas_call(
        functools.partial(_mm_res_kernel, midx=mod_k),
        grid=(N // tn, M // tm),
        in_specs=[pl.BlockSpec((tm, K), lambda j, i: (i, 0)),
                  pl.BlockSpec((None, K, tn), lambda j, i: (layer, 0, j), pipeline_mode=pl.Buffered(1)),
                  pl.BlockSpec((tm, tn), lambda j, i: (i, j)),
                  pl.BlockSpec((1, N_MOD, tn), lambda j, i: (midx(i), 0, j))],
        out_specs=pl.BlockSpec((tm, tn), lambda j, i: (i, j)),
        out_shape=jax.ShapeDtypeStruct((M, N), F32),
        scratch_shapes=[pltpu.VMEM((K, tn), BF16)],
        input_output_aliases={2: 0},
        compiler_params=_cp(("arbitrary", "arbitrary")),
        name="mm_res",
    )(a, w3, resid, mods)


def _mm_qkv(a, w3, layer, tabs, tm, tn, n_q_tiles, n_rope_tiles, q_scale, quarter):
    M, K = a.shape
    N = w3.shape[2]
    nblk = tn // LANES
    if tabs is None:
        tabs = jnp.zeros((1, 3, 8, LANES), F32)
        tab_spec = pl.BlockSpec((1, 3, 8, LANES), lambda j, i: (0, 0, 0, 0))
    else:
        tab_spec = pl.BlockSpec((1, 3, tm, LANES), lambda j, i: (jnp.minimum(j // n_q_tiles, 1), 0, i, 0))
    return pl.pallas_call(
        functools.partial(_mm_qkv_kernel, n_rope_tiles=n_rope_tiles, n_q_tiles=n_q_tiles,
                          q_scale=q_scale, quarter=quarter),
        grid=(N // tn, M // tm),
        in_specs=[pl.BlockSpec((tm, K), lambda j, i: (i, 0)),
                  pl.BlockSpec((None, K, tn), lambda j, i: (layer, 0, j), pipeline_mode=pl.Buffered(1)),
                  tab_spec],
        out_specs=pl.BlockSpec((nblk, tm, LANES), lambda j, i: (j, i, 0)),
        out_shape=jax.ShapeDtypeStruct((N // LANES, M, LANES), BF16),
        scratch_shapes=[pltpu.VMEM((K, tn), BF16)],
        compiler_params=_cp(("arbitrary", "arbitrary")),
        name="mm_qkv",
    )(a, w3, tabs)


def _rope_tables(n_lat, n_batch, n_ctx_rows, head_dim, q_scale):
    half, quarter = head_dim // 2, head_dim // 4
    pos = np.arange(n_lat)
    row, col = (pos // GRID_W).astype(np.float32), (pos % GRID_W).astype(np.float32)
    lane = np.arange(LANES) % head_dim
    inv = jnp.asarray(ROPE_THETA, F32) ** (-jnp.arange(quarter, dtype=F32) / quarter)
    p = jnp.where(jnp.asarray(lane // half == 0)[None, :], jnp.asarray(row)[:, None], jnp.asarray(col)[:, None])
    ang = p * inv[np.asarray((lane % half) % quarter)][None, :]
    first = jnp.asarray((lane % half) < quarter)[None, :]
    cos, sin = jnp.cos(ang), jnp.sin(ang)
    t = jnp.stack([cos, jnp.where(first, -sin, 0.0), jnp.where(first, 0.0, sin)])
    t = jnp.tile(t, (1, n_batch, 1))
    ident = jnp.stack([jnp.ones((n_ctx_rows, LANES), F32), jnp.zeros((n_ctx_rows, LANES), F32),
                       jnp.zeros((n_ctx_rows, LANES), F32)])
    t = jnp.concatenate([t, ident], axis=1)
    return jnp.stack([t * q_scale, t])


def _lane_fold(x, op, acc):
    for j in range(x.shape[1] // LANES):
        acc = op(acc, x[:, j * LANES:(j + 1) * LANES])
    return acc


def _da_kernel(lam_ref, q_ref, kc_ref, vc_ref, *rest, has_lat, out_scale, kchunk):
    if has_lat:
        kl_ref, vl_ref, g_ref, o_ref, s0_scr, s1_scr = rest
    else:
        g_ref, _, o_ref, s0_scr, s1_scr = rest
    s_scr = (s0_scr, s1_scr)
    lam = lam_ref[0]
    tq = q_ref.shape[1]
    C = kc_ref.shape[1]
    chunks = [(kc_ref, vc_ref, 0, C, 0)]
    if has_lat:
        chunks += [(kl_ref, vl_ref, j * kchunk, kchunk, C + j * kchunk) for j in range(kl_ref.shape[1] // kchunk)]

    rsub = math.gcd(tq, DA_STATE_ROWS)
    row_blocks = [slice(r, r + rsub) for r in range(0, tq, rsub)]

    def score_chunk(m, chunk, mrun):
        k_ref, _, off, n, col = chunk
        s = _dot_t(q_ref[m], k_ref[m, off:off + n, :])
        s_scr[m][:, col:col + n] = s
        return [_lane_fold(s[rows], jnp.maximum, mr) for rows, mr in zip(row_blocks, mrun)]

    def prob_chunk(m, chunk, mx, state):
        _, v_ref, off, n, col = chunk
        lrun, o = state
        p, lnew = [], []
        for rows, mxr, lr in zip(row_blocks, mx, lrun):
            pr = jnp.exp2(s_scr[m][rows, col:col + n] - mxr)
            lnew.append(_lane_fold(pr, jnp.add, lr))
            p.append(pr.astype(BF16))
        v = jnp.concatenate([v_ref[0, off:off + n, :], v_ref[1, off:off + n, :]], axis=1)
        return lnew, o + _dot(jnp.concatenate(p, axis=0), v)

    def row_max(mrun):
        return [jnp.max(mr, axis=-1, keepdims=True) for mr in mrun]

    def row_sum(lrun):
        return jnp.concatenate([jnp.sum(lr, axis=-1, keepdims=True) for lr in lrun], axis=0)

    m_init = [jnp.full((rsub, LANES), -jnp.inf, F32) for _ in row_blocks]
    st_init = ([jnp.zeros((rsub, LANES), F32) for _ in row_blocks], jnp.zeros((tq, 2 * LANES), F32))
    mrun0 = m_init
    for ch in chunks:
        mrun0 = score_chunk(0, ch, mrun0)
    mx0 = row_max(mrun0)
    mrun1, st0 = m_init, st_init
    for ch in chunks:
        mrun1 = score_chunk(1, ch, mrun1)
        st0 = prob_chunk(0, ch, mx0, st0)
    mx1 = row_max(mrun1)
    st1 = st_init
    for ch in chunks:
        st1 = prob_chunk(1, ch, mx1, st1)
    o = st0[1] * (1.0 / row_sum(st0[0])) - st1[1] * (lam / row_sum(st1[0]))
    o_ref[...] = (_norm_rows(o, g_ref[...]) * out_scale).astype(BF16)


def _da_attention(qkv, lam, subln_g, out_scale, n_batch, S, C, H, tq):
    R = qkv.shape[1]
    n_lat = n_batch * S
    d2 = 2 * DA_HEAD_DIM
    g = subln_g.reshape(1, d2)
    smem = pl.BlockSpec(memory_space=pltpu.SMEM)
    tql = _tile(S, tq, 8)
    kchunk = _tile(S, 512)
    out = pl.pallas_call(
        functools.partial(_da_kernel, has_lat=True, out_scale=out_scale, kchunk=kchunk),
        grid=(n_batch, H, S // tql),
        in_specs=[smem,
                  pl.BlockSpec((2, tql, LANES), lambda b, h, i: (h, b * (S // tql) + i, 0)),
                  pl.BlockSpec((2, C, LANES), lambda b, h, i: (H + h, n_lat // C + b, 0)),
                  pl.BlockSpec((2, C, LANES), lambda b, h, i: (2 * H + h, n_lat // C + b, 0)),
                  pl.BlockSpec((2, S, LANES), lambda b, h, i: (H + h, b, 0)),
                  pl.BlockSpec((2, S, LANES), lambda b, h, i: (2 * H + h, b, 0)),
                  pl.BlockSpec((1, d2), lambda b, h, i: (0, 0))],
        out_specs=pl.BlockSpec((tql, d2), lambda b, h, i: (b * (S // tql) + i, h)),
        out_shape=jax.ShapeDtypeStruct((R, H * d2), BF16),
        scratch_shapes=[pltpu.VMEM((tql, C + S), F32), pltpu.VMEM((tql, C + S), F32)],
        compiler_params=_cp(("arbitrary", "arbitrary", "arbitrary")),
        name="da_attn_lat",
    )(lam, qkv, qkv, qkv, qkv, qkv, g)
    tqc = _tile(C, tq, 8)
    return pl.pallas_call(
        functools.partial(_da_kernel, has_lat=False, out_scale=out_scale, kchunk=kchunk),
        grid=(n_batch, H, C // tqc),
        in_specs=[smem,
                  pl.BlockSpec((2, tqc, LANES), lambda b, h, i: (h, (n_lat + b * C) // tqc + i, 0)),
                  pl.BlockSpec((2, C, LANES), lambda b, h, i: (H + h, n_lat // C + b, 0)),
                  pl.BlockSpec((2, C, LANES), lambda b, h, i: (2 * H + h, n_lat // C + b, 0)),
                  pl.BlockSpec((1, d2), lambda b, h, i: (0, 0)),
                  pl.BlockSpec(memory_space=pl.ANY)],
        out_specs=pl.BlockSpec((tqc, d2), lambda b, h, i: ((n_lat + b * C) // tqc + i, h)),
        out_shape=jax.ShapeDtypeStruct((R, H * d2), BF16),
        scratch_shapes=[pltpu.VMEM((tqc, C), F32), pltpu.VMEM((tqc, C), F32)],
        input_output_aliases={5: 0},
        compiler_params=_cp(("arbitrary", "arbitrary", "arbitrary")),
        name="da_attn_ctx",
    )(lam, qkv, qkv, qkv, g, out)


def _na_geometry(rows):
    kh, rq = min(NA_KH, rows), NA_ROWS_PER_TILE
    kr = kh + rq
    assert rows % rq == 0 and rows >= kr and rows // rq >= 3
    return kh, rq, kr


def _na_bias(rpb, rows):
    kh, rq, kr = _na_geometry(rows)
    W, kw = GRID_W, NA_KW
    n_tiles = rows // rq
    H, n_dr = rpb.shape[0], rpb.shape[1]
    assert LANES % W == 0 and (kr * W) % LANES == 0

    def row_geometry(i):
        r = i * rq + np.arange(rq)
        k = int(np.clip(i * rq - kh // 2, 0, rows - kr)) + np.arange(kr)
        rs = np.clip(r - kh // 2, 0, rows - kh)
        valid = (k[None, :] >= rs[:, None]) & (k[None, :] < rs[:, None] + kh)
        return np.where(valid, k[None, :] - r[:, None] + (NA_KH - 1), n_dr)

    pats = [row_geometry(0), row_geometry(1), row_geometry(n_tiles - 1)]
    for i in range(1, n_tiles - 1):
        assert (row_geometry(i) == pats[1]).all()
    idx = jnp.asarray(np.stack(pats).reshape(-1), I32)

    L = 2 * W - 1
    u = jnp.pad(rpb.astype(F32), ((0, 0), (0, 0), (0, L - rpb.shape[2])))
    t = jnp.tile(u, (1, 1, W))[:, :, :W * (L - 1)].reshape(H, rpb.shape[1], W, L - 1)[..., kw - 1:kw - 1 + W]
    c = np.arange(W)
    cs = np.clip(c - kw // 2, 0, W - kw)
    vcol = (c[None, :] >= cs[:, None]) & (c[None, :] < cs[:, None] + kw)
    t = jnp.where(vcol[None, None], t * LOG2E, NEG_BIG)
    t = jnp.concatenate([t, jnp.full((H, 1, W, W), NEG_BIG, F32)], axis=1)
    per = LANES // W
    tab = jnp.stack([jnp.pad(t, ((0, 0), (0, 0), (0, 0), (s * W, LANES - (s + 1) * W))) for s in range(per)])
    return tab, idx


def _na_kernel(q_ref, kc_ref, vc_ref, *rest, has_loc, rows):
    if has_loc:
        k_ref, v_ref, idx_ref, tab_ref, o_ref = rest
        kh, rq, kr = _na_geometry(rows)
        i = pl.program_id(2)
        n_tiles = rows // rq
        start = jnp.clip(i * rq - kh // 2, 0, rows - kr) * GRID_W
        start = pl.multiple_of(start, GRID_W)
        pat = jnp.where(i == 0, 0, jnp.where(i == n_tiles - 1, 2, 1))
        per = tab_ref.shape[0]

        def bias_of(hb):
            rows_out = []
            for qr in range(rq):
                blocks = []
                for j in range(kr // per):
                    piece = None
                    for s in range(per):
                        dr = idx_ref[(pat * rq + qr) * kr + j * per + s]
                        term = tab_ref[s, hb, pl.ds(dr, 1)][0]
                        piece = term if piece is None else piece + term
                    blocks.append(piece)
                rows_out.append(jnp.concatenate(blocks, axis=1))
            return jnp.concatenate(rows_out, axis=0)
    else:
        _, o_ref = rest
    for hb in range(q_ref.shape[0]):
        q = q_ref[hb]
        sc = _dot_t(q, kc_ref[hb])
        mx = jnp.max(sc, axis=-1, keepdims=True)
        if has_loc:
            sl = _dot_t(q, k_ref[hb, pl.ds(start, kr * GRID_W), :]) + bias_of(hb)
            mx = jnp.maximum(mx, jnp.max(sl, axis=-1, keepdims=True))
        pc = jnp.exp2(sc - mx)
        l = jnp.sum(pc, axis=-1, keepdims=True)
        o = _dot(pc.astype(BF16), vc_ref[hb])
        if has_loc:
            pl_ = jnp.exp2(sl - mx)
            l = l + jnp.sum(pl_, axis=-1, keepdims=True)
            o = o + _dot(pl_.astype(BF16), v_ref[hb, pl.ds(start, kr * GRID_W), :])
        o_ref[:, hb * LANES:(hb + 1) * LANES] = (o * (1.0 / l)).astype(BF16)


def _na_attention(qkv, bias, n_batch, S, C, H):
    R = qkv.shape[1]
    n_lat = n_batch * S
    rows = S // GRID_W
    kh, rq, kr = _na_geometry(rows)
    tq, n_tiles = rq * GRID_W, rows // rq
    d = NA_HEAD_DIM
    tab, idx = bias
    hb = math.gcd(H, NA_HEADS_PER_STEP)
    nhb = H // hb
    out = pl.pallas_call(
        functools.partial(_na_kernel, has_loc=True, rows=rows),
        grid=(n_batch, nhb, n_tiles),
        in_specs=[pl.BlockSpec((hb, tq, LANES), lambda b, h, i: (h, b * n_tiles + i, 0)),
                  pl.BlockSpec((hb, C, LANES), lambda b, h, i: (nhb + h, n_lat // C + b, 0)),
                  pl.BlockSpec((hb, C, LANES), lambda b, h, i: (2 * nhb + h, n_lat // C + b, 0)),
                  pl.BlockSpec((hb, S, LANES), lambda b, h, i: (nhb + h, b, 0)),
                  pl.BlockSpec((hb, S, LANES), lambda b, h, i: (2 * nhb + h, b, 0)),
                  pl.BlockSpec(memory_space=pltpu.SMEM),
                  pl.BlockSpec((tab.shape[0], hb) + tab.shape[2:], lambda b, h, i: (0, h, 0, 0, 0))],
        out_specs=pl.BlockSpec((tq, hb * d), lambda b, h, i: (b * n_tiles + i, h)),
        out_shape=jax.ShapeDtypeStruct((R, H * d), BF16),
        compiler_params=_cp(("arbitrary", "arbitrary", "arbitrary")),
        name="na_attn_lat",
    )(qkv, qkv, qkv, qkv, qkv, idx, tab)
    return pl.pallas_call(
        functools.partial(_na_kernel, has_loc=False, rows=rows),
        grid=(n_batch, nhb, 1),
        in_specs=[pl.BlockSpec((hb, C, LANES), lambda b, h, i: (h, n_lat // C + b, 0)),
                  pl.BlockSpec((hb, C, LANES), lambda b, h, i: (nhb + h, n_lat // C + b, 0)),
                  pl.BlockSpec((hb, C, LANES), lambda b, h, i: (2 * nhb + h, n_lat // C + b, 0)),
                  pl.BlockSpec(memory_space=pl.ANY)],
        out_specs=pl.BlockSpec((C, hb * d), lambda b, h, i: (n_lat // C + b, h)),
        out_shape=jax.ShapeDtypeStruct((R, H * d), BF16),
        input_output_aliases={3: 0},
        compiler_params=_cp(("arbitrary", "arbitrary", "arbitrary")),
        name="na_attn_ctx",
    )(qkv, qkv, qkv, out)


def _sw_kernel(sink_ref, q_ref, kc_ref, vc_ref, *rest, has_loc, tq, band, S, G):
    if has_loc:
        k_ref, v_ref, o_ref = rest
    else:
        _, o_ref = rest
    hd = SW_HEAD_DIM
    per_blk = LANES // hd
    g2n = G // per_blk
    C = kc_ref.shape[1]
    c, i = pl.program_id(1), pl.program_id(2)
    lane = lax.broadcasted_iota(I32, (1, LANES), 1)

    def block_diag(x, par):
        sw = jnp.concatenate([x[:, hd:], x[:, :hd]], axis=1)
        zero = jnp.zeros_like(x)
        top = jnp.where(lane < hd, x if par == 0 else sw, zero)
        bot = jnp.where(lane >= hd, sw if par == 0 else x, zero)
        return jnp.concatenate([top, bot], axis=0)

    if has_loc:
        bstart = pl.multiple_of(jnp.clip(i * tq - SW_WINDOW, 0, S - band), LANES)
        kb = k_ref[0, pl.ds(bstart, band), :]
        vb = v_ref[0, pl.ds(bstart, band), :]
        qpos = i * tq + lax.broadcasted_iota(I32, (tq, 1), 0)
        kpos = bstart + lax.broadcasted_iota(I32, (1, band), 1)
        valid = jnp.abs(kpos - qpos) <= SW_WINDOW

    for par in range(2):
        kc_bd, vc_bd = block_diag(kc_ref[0], par), block_diag(vc_ref[0], par)
        if has_loc:
            kl_bd, vl_bd = block_diag(kb, par), block_diag(vb, par)
        for g2 in range(g2n):
            blk = par * g2n + g2
            q = q_ref[blk]
            sc2 = _dot_t(q, kc_bd)
            if has_loc:
                sl2 = _dot_t(q, kl_bd)
            pcs, pls, rls = [], [], []
            for hh in range(per_blk):
                sink = sink_ref[(2 * c + par) * G + g2 * per_blk + hh] * LOG2E
                sc = sc2[:, hh * C:(hh + 1) * C]
                mx = jnp.maximum(jnp.max(sc, axis=-1, keepdims=True), sink)
                if has_loc:
                    sl = jnp.where(valid, sl2[:, hh * band:(hh + 1) * band], NEG_BIG)
                    mx = jnp.maximum(mx, jnp.max(sl, axis=-1, keepdims=True))
                pc = jnp.exp2(sc - mx)
                l = jnp.sum(pc, axis=-1, keepdims=True) + jnp.exp2(sink - mx)
                pcs.append(pc)
                if has_loc:
                    pl_ = jnp.exp2(sl - mx)
                    l = l + jnp.sum(pl_, axis=-1, keepdims=True)
                    pls.append(pl_)
                rls.append(1.0 / l)
            o = _dot(jnp.concatenate(pcs, axis=1).astype(BF16), vc_bd)
            if has_loc:
                o = o + _dot(jnp.concatenate(pls, axis=1).astype(BF16), vl_bd)
            o = o * jnp.where(lane < hd, rls[0], rls[1])
            o_ref[:, blk * LANES:(blk + 1) * LANES] = o.astype(BF16)


def _sw_attention(qkv, sink, n_batch, S, C, H, tq):
    R = qkv.shape[1]
    n_lat = n_batch * S
    hd, KV = SW_HEAD_DIM, SW_KV_HEADS
    G = H // KV
    nqb, nkb = H * hd // LANES, KV * hd // LANES
    qpk = 2 * G * hd // LANES
    assert KV % 2 == 0 and LANES // hd == 2 and G % 2 == 0
    tq = _tile(S, tq)
    band = tq + 2 * SW_WINDOW
    assert band <= S and SW_WINDOW % LANES == 0
    smem = pl.BlockSpec(memory_space=pltpu.SMEM)
    out = pl.pallas_call(
        functools.partial(_sw_kernel, has_loc=True, tq=tq, band=band, S=S, G=G),
        grid=(n_batch, KV // 2, S // tq),
        in_specs=[smem,
                  pl.BlockSpec((qpk, tq, LANES), lambda b, c, i: (c, b * (S // tq) + i, 0)),
                  pl.BlockSpec((1, C, LANES), lambda b, c, i: (nqb + c, n_lat // C + b, 0)),
                  pl.BlockSpec((1, C, LANES), lambda b, c, i: (nqb + nkb + c, n_lat // C + b, 0)),
                  pl.BlockSpec((1, S, LANES), lambda b, c, i: (nqb + c, b, 0)),
                  pl.BlockSpec((1, S, LANES), lambda b, c, i: (nqb + nkb + c, b, 0))],
        out_specs=pl.BlockSpec((tq, qpk * LANES), lambda b, c, i: (b * (S // tq) + i, c)),
        out_shape=jax.ShapeDtypeStruct((R, H * hd), BF16),
        compiler_params=_cp(("arbitrary", "arbitrary", "arbitrary")),
        name="sw_attn_lat",
    )(sink, qkv, qkv, qkv, qkv, qkv)
    return pl.pallas_call(
        functools.partial(_sw_kernel, has_loc=False, tq=C, band=0, S=S, G=G),
        grid=(n_batch, KV // 2, 1),
        in_specs=[smem,
                  pl.BlockSpec((qpk, C, LANES), lambda b, c, i: (c, n_lat // C + b, 0)),
                  pl.BlockSpec((1, C, LANES), lambda b, c, i: (nqb + c, n_lat // C + b, 0)),
                  pl.BlockSpec((1, C, LANES), lambda b, c, i: (nqb + nkb + c, n_lat // C + b, 0)),
                  pl.BlockSpec(memory_space=pl.ANY)],
        out_specs=pl.BlockSpec((C, qpk * LANES), lambda b, c, i: (n_lat // C + b, c)),
        out_shape=jax.ShapeDtypeStruct((R, H * hd), BF16),
        input_output_aliases={4: 0},
        compiler_params=_cp(("arbitrary", "arbitrary", "arbitrary")),
        name="sw_attn_ctx",
    )(sink, qkv, qkv, qkv, out)


def _moe_plan(eidx, tme):
    e_flat = eidx[:2].reshape(-1)
    onehot = (e_flat[:, None] == jnp.arange(N_EXPERTS, dtype=I32)[None, :]).astype(I32)
    csum = jnp.cumsum(onehot, axis=0)
    rank = jnp.sum(csum * onehot, axis=1) - 1
    cnt = csum[-1]
    pcnt = (cnt + tme - 1) // tme * tme
    ends = jnp.cumsum(pcnt)
    offs = ends - pcnt
    dest = (offs[e_flat] + rank).astype(I32)
    pad_row = jnp.where(cnt % tme != 0, ends - tme, -1).astype(I32)
    return dest, (pcnt // tme).astype(I32), (offs // tme).astype(I32), pad_row


def _work_list(tiles_e, tile_base, n_chunks, n_items_max):
    items_e = tiles_e * n_chunks
    ends = jnp.cumsum(items_e)
    nw = ends[-1]
    w = jnp.minimum(jnp.arange(n_items_max, dtype=I32), nw - 1)
    e = jnp.sum((w[:, None] >= ends[None, :]).astype(I32), axis=1)
    local = w - (ends[e] - items_e[e])
    te = jnp.maximum(tiles_e[e], 1)
    chunk, r = local // te, local % te
    return (e.astype(I32), chunk.astype(I32), (tile_base[e] + r).astype(I32),
            (r == 0).astype(I32), nw.astype(I32).reshape(1))


def _token_copy(src, src_tok, dst, dst_tok, k, sem):
    def first_row(tok):
        return tok * k if isinstance(tok, int) else pl.multiple_of(tok * k, k)

    return pltpu.make_async_copy(src.at[pl.ds(first_row(src_tok), k)], dst.at[pl.ds(first_row(dst_tok), k)], sem)


def _scatter_rows_kernel(dest_ref, pad_ref, hp_ref, o_hbm, zbuf, sem, zsem, *, ts, n_rows, tme, k):
    def zero_copy(e):
        row = pl.multiple_of(pad_ref[e] * k, 8)
        return pltpu.make_async_copy(zbuf, o_hbm.at[pl.ds(row, tme * k)], zsem)

    @pl.when(pl.program_id(0) == 0)
    def _():
        zbuf[...] = jnp.zeros_like(zbuf)
        for e in range(N_EXPERTS):
            pl.when(pad_ref[e] >= 0)(lambda e=e: zero_copy(e).start())
        for e in range(N_EXPERTS):
            pl.when(pad_ref[e] >= 0)(lambda e=e: zero_copy(e).wait())

    base = pl.program_id(0) * ts

    def issue(r, carry):
        for slot in range(2):
            _token_copy(hp_ref, r, o_hbm, dest_ref[slot * n_rows + base + r], k, sem).start()
        return carry

    def drain(r, carry):
        for slot in range(2):
            _token_copy(hp_ref, r, o_hbm, 0, k, sem).wait()
        return carry

    lax.fori_loop(0, ts, issue, 0)
    lax.fori_loop(0, ts, drain, 0)


def _scatter_rows(hp, dest, pad_row, n_rows, tme, p_max):
    k = hp.shape[0] // n_rows
    ts = _tile(n_rows, 256, 8)
    return pl.pallas_call(
        functools.partial(_scatter_rows_kernel, ts=ts, n_rows=n_rows, tme=tme, k=k),
        grid_spec=pltpu.PrefetchScalarGridSpec(
            num_scalar_prefetch=2, grid=(n_rows // ts,),
            in_specs=[pl.BlockSpec((ts * k, LANES), lambda i, d, p: (i, 0))],
            out_specs=pl.BlockSpec(memory_space=pl.ANY),
            scratch_shapes=[pltpu.VMEM((tme * k, LANES), hp.dtype), pltpu.SemaphoreType.DMA(()),
                            pltpu.SemaphoreType.DMA(())]),
        out_shape=jax.ShapeDtypeStruct((p_max * k, LANES), hp.dtype),
        compiler_params=_cp(("arbitrary",)),
        name="moe_scatter",
    )(dest, pad_row, hp)


def _silu(g):
    return g / (1.0 + jnp.exp(-g))


def _moe_up_kernel(ie, ic, it, ifirst, nw, hp_ref, wg_ref, wu_ref, a_ref, wgb, wub):
    w = pl.program_id(0)

    @pl.when(w < nw[0])
    def _():
        @pl.when(ifirst[w] == 1)
        def _():
            wgb[...] = wg_ref[...].astype(BF16)
            wub[...] = wu_ref[...].astype(BF16)

        hp = jnp.concatenate(_load_token_major(hp_ref, a_ref.shape[0]), axis=1)
        half = hp.shape[1]
        lo, hi = (v.astype(BF16) for v in _unpack_bf16_pair(hp))
        g = _dot(lo, wgb[:half]) + _dot(hi, wgb[half:])
        u = _dot(lo, wub[:half]) + _dot(hi, wub[half:])
        a_ref[...] = (_silu(g) * u).astype(BF16)


def _moe_down_kernel(ie, ic, it, ifirst, nw, a_ref, wd_ref, y_ref, wdb):
    w = pl.program_id(0)

    @pl.when(w < nw[0])
    def _():
        @pl.when(ifirst[w] == 1)
        def _():
            wdb[...] = wd_ref[...].astype(BF16)

        half = y_ref.shape[1]
        width = min(half, MXU_COLS)
        a = a_ref[...]
        for c0 in range(0, half, width):
            lo = _dot(a, wdb[:, c0:c0 + width])
            hi = _dot(a, wdb[:, half + c0:half + c0 + width])
            y_ref[:, c0:c0 + width] = _pack_bf16_pair(lo, hi)


def _pack_bf16_pair(lo, hi):
    lo_bits = pltpu.bitcast(lo.astype(BF16).astype(F32), U32)
    hi_bits = pltpu.bitcast(hi.astype(BF16).astype(F32), U32)
    return (lo_bits >> 16) | (hi_bits & jnp.uint32(0xFFFF0000))


def _unpack_bf16_pair(p):
    return pltpu.bitcast(p << 16, F32), pltpu.bitcast(p & jnp.uint32(0xFFFF0000), F32)


def _row_copy(src_hbm, src_row, dst, dst_row, sem):
    return pltpu.make_async_copy(src_hbm.at[pl.ds(src_row, 1)], dst.at[pl.ds(dst_row, 1)], sem)


def _combine_kernel(dest_ref, y_hbm, x_ref, mod_ref, gate_ref, o_ref, buf, sem, *, tc, n_rows, mod_k):
    base = pl.program_id(0) * tc

    def issue(r, carry):
        for slot in range(2):
            _row_copy(y_hbm, dest_ref[slot * n_rows + base + r], buf.at[slot], r, sem).start()
        return carry

    def drain(r, carry):
        for slot in range(2):
            _row_copy(y_hbm, 0, buf.at[slot], r, sem).wait()
        return carry

    lax.fori_loop(0, tc, issue, 0)
    lax.fori_loop(0, tc, drain, 0)
    half = buf.shape[2]
    g0, g1 = gate_ref[:, 0:1], gate_ref[:, 1:2]
    width = min(half, 4 * LANES)
    for c0 in range(0, half, width):
        cols = slice(c0, c0 + width)
        y0_lo, y0_hi = _unpack_bf16_pair(buf[0, :, cols])
        y1_lo, y1_hi = _unpack_bf16_pair(buf[1, :, cols])
        for off, y in ((0, g0 * y0_lo + g1 * y1_lo), (half, g0 * y0_hi + g1 * y1_hi)):
            out_cols = slice(off + c0, off + c0 + width)
            o_ref[:, out_cols] = x_ref[:, out_cols] + mod_ref[0, mod_k:mod_k + 1, out_cols] * y


def _moe(X, hp, eidx, gw, w_gate, w_up, w_down, layer, mods, mod_k, midx):
    R, D = X.shape
    d_ff = w_gate.shape[3]
    tme = MOE_ROW_TILE
    p_max = -(-(2 * R + N_EXPERTS * (tme - 1)) // tme) * tme
    t_max = p_max // tme
    dest, tiles_e, tile_base, pad_row = _moe_plan(eidx, tme)
    hs = _scatter_rows(hp, dest, pad_row, R, tme, p_max)
    kp = D // 2 // LANES

    fc = _tile(d_ff, 768)
    n_f = d_ff // fc
    w_spec = pl.BlockSpec((None, None, D, fc), lambda w, ie, ic, it, fi, nw: (layer, ie[w], 0, ic[w]),
                          pipeline_mode=pl.Buffered(1))
    items = _work_list(tiles_e, tile_base, n_f, n_f * t_max)
    a = pl.pallas_call(
        _moe_up_kernel,
        grid_spec=pltpu.PrefetchScalarGridSpec(
            num_scalar_prefetch=5, grid=(n_f * t_max,),
            in_specs=[pl.BlockSpec((tme * kp, LANES), lambda w, ie, ic, it, fi, nw: (it[w], 0)),
                      w_spec, w_spec],
            out_specs=pl.BlockSpec((tme, fc), lambda w, ie, ic, it, fi, nw: (it[w], ic[w])),
            scratch_shapes=[pltpu.VMEM((D, fc), BF16), pltpu.VMEM((D, fc), BF16)]),
        out_shape=jax.ShapeDtypeStruct((p_max, d_ff), BF16),
        compiler_params=_cp(("arbitrary",)),
        name="moe_up",
    )(*items, hs, w_gate, w_up)

    items = _work_list(tiles_e, tile_base, 1, t_max)
    y = pl.pallas_call(
        _moe_down_kernel,
        grid_spec=pltpu.PrefetchScalarGridSpec(
            num_scalar_prefetch=5, grid=(t_max,),
            in_specs=[pl.BlockSpec((tme, d_ff), lambda w, ie, ic, it, fi, nw: (it[w], 0)),
                      pl.BlockSpec((None, None, d_ff, D), lambda w, ie, ic, it, fi, nw: (layer, ie[w], 0, 0))],
            out_specs=pl.BlockSpec((tme, D // 2), lambda w, ie, ic, it, fi, nw: (it[w], 0)),
            scratch_shapes=[pltpu.VMEM((d_ff, D), BF16)]),
        out_shape=jax.ShapeDtypeStruct((p_max, D // 2), U32),
        compiler_params=_cp(("arbitrary",)),
        name="moe_down",
    )(*items, a, w_down)

    tc = _tile(R, 128, 8)
    mod_of_tile = midx(tc)
    return pl.pallas_call(
        functools.partial(_combine_kernel, tc=tc, n_rows=R, mod_k=mod_k),
        grid_spec=pltpu.PrefetchScalarGridSpec(
            num_scalar_prefetch=1, grid=(R // tc,),
            in_specs=[pl.BlockSpec(memory_space=pl.ANY),
                      pl.BlockSpec((tc, D), lambda i, d: (i, 0)),
                      pl.BlockSpec((1, N_MOD, D), lambda i, d: (mod_of_tile(i), 0, 0)),
                      pl.BlockSpec((tc, 2), lambda i, d: (i, 0))],
            out_specs=pl.BlockSpec((tc, D), lambda i, d: (i, 0)),
            scratch_shapes=[pltpu.VMEM((2, tc, D // 2), U32), pltpu.SemaphoreType.DMA(())]),
        out_shape=jax.ShapeDtypeStruct((R, D), F32),
        input_output_aliases={2: 0},
        compiler_params=_cp(("arbitrary",)),
        name="moe_combine",
    )(dest, y, X, mods, gw[:2].T)


def kernel(x, c, ctx, c_ctx, cond_down, mod_w, mod_b, norm_g, final_norm_g, da_w_qkv, da_w_o, da_lambda, da_subln_g, na_w_qkv, na_w_o, na_rpb, sw_w_qkv, sw_w_o, sw_sink, router_w, router_bias, moe_w_gate, moe_w_up, moe_w_down):
    B, S, D = x.shape
    C = ctx.shape[1]
    depth = mod_w.shape[0]
    n_lat, n_ctx = B * S, B * C
    R = n_lat + n_ctx

    def midx(tm):
        assert S % tm == 0 and n_ctx % tm == 0
        return _mod_index_fn(n_lat // tm, S // tm, B)

    X = jnp.concatenate([x.reshape(n_lat, D), ctx.reshape(n_ctx, D)], axis=0)

    cond_rows = 16
    cin = jnp.concatenate([c, c_ctx[None, :], jnp.zeros((cond_rows - B - 1, D), F32)], axis=0)
    cin = jax.nn.silu(cin).astype(BF16)
    cond = _mm_bias(cin, cond_down[None], 0, jnp.zeros((1, cond_down.shape[1]), F32), BF16)

    rw_pad = jnp.zeros((D, LANES), F32).at[:, :N_EXPERTS].set(router_w).astype(BF16)
    rb = router_bias.astype(F32).reshape(N_EXPERTS, 1)

    tm_row = _tile(math.gcd(S, n_ctx), 256, 8)
    tm_mm = _tile(math.gcd(S, n_ctx), 512, 8)
    da_h = da_w_qkv.shape[2] // (6 * DA_HEAD_DIM)
    na_h = na_w_qkv.shape[2] // (3 * NA_HEAD_DIM)
    sw_h = sw_sink.shape[1]
    tabs = {}

    for i in range(depth):
        mods = _mm_bias(cond, mod_w, i, mod_b[i][None, :], F32)[:B + 1].reshape(B + 1, N_MOD, D)
        h = _norm_mod(X, norm_g[i, 0], mods, 0, 1, tm_row, midx(tm_row))
        kind, j = i % N_MIXERS, i // N_MIXERS
        if kind == 0:
            d = DA_HEAD_DIM
            qd = da_h * 2 * d
            tn = _tile(qd, 1024)
            if "da" not in tabs:
                tabs["da"] = _rope_tables(S, B, n_ctx, d, d ** -0.5 * LOG2E)
            qkv = _mm_qkv(h, da_w_qkv, j, tabs["da"], tm_mm, tn, qd // tn, 2 * qd // tn, 1.0, d // 4)
            lam_init = 0.8 - 0.6 * math.exp(-0.3 * i)
            lp = da_lambda[j].astype(F32)
            lam = (jnp.exp(jnp.sum(lp[0] * lp[1])) - jnp.exp(jnp.sum(lp[2] * lp[3])) + lam_init).reshape(1)
            o = _da_attention(qkv, lam, da_subln_g[j], 1.0 - lam_init, B, S, C, da_h, 512)
            w_o = da_w_o
        elif kind == 1:
            d = NA_HEAD_DIM
            tn = _tile(na_h * d, 1024)
            qkv = _mm_qkv(h, na_w_qkv, j, None, tm_mm, tn, na_h * d // tn, 0, d ** -0.5 * LOG2E, 0)
            o = _na_attention(qkv, _na_bias(na_rpb[j], S // GRID_W), B, S, C, na_h)
            w_o = na_w_o
        else:
            d = SW_HEAD_DIM
            nq, nkv = sw_h * d, SW_KV_HEADS * d
            tn = _tile(math.gcd(nq, nkv), 512)
            if "sw" not in tabs:
                tabs["sw"] = _rope_tables(S, B, n_ctx, d, d ** -0.5 * LOG2E)
            qkv = _mm_qkv(h, sw_w_qkv, j, tabs["sw"], tm_mm, tn, nq // tn, (nq + nkv) // tn, 1.0, d // 4)
            o = _sw_attention(qkv, sw_sink[j].astype(F32), B, S, C, sw_h, 256)
            w_o = sw_w_o
        X = _mm_res(o, w_o, j, X, mods, 2, tm_mm, midx(tm_mm))
        hp, eidx, gw = _norm_router(X, norm_g[i, 1], mods, 3, 4, rw_pad, rb, tm_row, midx(tm_row))
        X = _moe(X, hp, eidx, gw, moe_w_gate, moe_w_up, moe_w_down, i, mods, 5, midx)

    tm_fin = _tile(S, 256, 8)
    return _final_norm(X, final_norm_g, n_lat, tm_fin).reshape(B, S, D)
```

```python
import functools
import math

import numpy as np
import jax
import jax.numpy as jnp
from jax import lax
from jax.experimental import pallas as pl
from jax.experimental.pallas import tpu as pltpu

F32 = jnp.float32
BF16 = jnp.bfloat16
U32 = jnp.uint32
I32 = jnp.int32

GRID_W = 64
N_MIXERS = 3
N_MOD = 6
RMS_EPS = 1e-6
ROPE_THETA = 10000.0
DA_HEAD_DIM = 128
NA_HEAD_DIM = 128
NA_KH = 8
NA_KW = 16
SW_HEAD_DIM = 64
SW_KV_HEADS = 8
SW_WINDOW = 128
N_EXPERTS = 16
N_GROUPS = 4
EXPERTS_PER_GROUP = N_EXPERTS // N_GROUPS

LANES = 128
NEG_BIG = -1e30
NA_ROWS_PER_TILE = 4
NA_HEADS_PER_STEP = 4
DA_STATE_ROWS = 128
LOG2E = math.log2(math.e)
MOE_ROW_TILE = 256
VMEM_MB = 56


def _cp(sem, vmem_mb=VMEM_MB, **kw):
    return pltpu.CompilerParams(dimension_semantics=sem, vmem_limit_bytes=vmem_mb << 20, **kw)


def _tile(n, pref, quantum=LANES):
    if n <= pref:
        return n
    t = (pref // quantum) * quantum
    while t > quantum and n % t:
        t -= quantum
    assert n % t == 0, (n, pref)
    return t


def _dot_t(a, b):
    return lax.dot_general(a, b, (((1,), (1,)), ((), ())), preferred_element_type=F32)


def _dot(a, b):
    return jnp.dot(a, b, preferred_element_type=F32)


def _mod_index_fn(n_lat_tiles, tiles_per_batch, n_batch):
    def f(i):
        return jnp.where(i < n_lat_tiles, i // tiles_per_batch, n_batch)
    return f


def _norm_rows(x, g):
    ms = jnp.mean(x * x, axis=-1, keepdims=True)
    return x * lax.rsqrt(ms + RMS_EPS) * g


def _norm_mod_kernel(x_ref, g_ref, mod_ref, h_ref, *, shift, scale):
    y = _norm_rows(x_ref[...], g_ref[...])
    h = y * (1.0 + mod_ref[0, scale:scale + 1, :]) + mod_ref[0, shift:shift + 1, :]
    h_ref[...] = h.astype(BF16)


def _top2_sum(a, b, c, d):
    hi1, lo1 = jnp.maximum(a, b), jnp.minimum(a, b)
    hi2, lo2 = jnp.maximum(c, d), jnp.minimum(c, d)
    return jnp.maximum(hi1, hi2) + jnp.maximum(jnp.minimum(hi1, hi2), jnp.maximum(lo1, lo2))


def _argmax_first(vals):
    best, idx = vals[0], jnp.zeros(vals[0].shape, I32)
    for k in range(1, len(vals)):
        upd = vals[k] > best
        idx = jnp.where(upd, k, idx)
        best = jnp.where(upd, vals[k], best)
    return idx, best


def _store_token_major(ref, x):
    n, k = x.shape[0], x.shape[1] // LANES
    for j in range(k):
        ref[pl.ds(j, n, stride=k), :] = x[:, j * LANES:(j + 1) * LANES]


def _load_token_major(ref, n):
    k = ref.shape[0] // n
    return [ref[pl.ds(j, n, stride=k), :] for j in range(k)]


def _norm_router_kernel(x_ref, g_ref, mod_ref, rw_ref, rb_ref, hp_ref, eidx_ref, gw_ref, *, shift, scale):
    y = _norm_rows(x_ref[...], g_ref[...])
    h = y * (1.0 + mod_ref[0, scale:scale + 1, :]) + mod_ref[0, shift:shift + 1, :]
    hb = h.astype(BF16)
    half = h.shape[1] // 2
    _store_token_major(hp_ref, _pack_bf16_pair(h[:, :half], h[:, half:]))

    logits = _dot(hb, rw_ref[...])
    lt = logits.T[:N_EXPERTS]
    aff = 1.0 / (1.0 + jnp.exp(-lt))
    sel = aff + rb_ref[...]
    sel_rows = [sel[e:e + 1] for e in range(N_EXPERTS)]
    aff_rows = [aff[e:e + 1] for e in range(N_EXPERTS)]
    epg = EXPERTS_PER_GROUP
    gscore = [_top2_sum(*sel_rows[g * epg:(g + 1) * epg]) for g in range(N_GROUPS)]
    bg, _ = _argmax_first(gscore)

    def pick(rows, k):
        out = rows[k]
        for g in range(1, N_GROUPS):
            out = jnp.where(bg == g, rows[g * epg + k], out)
        return out

    s_in = [pick(sel_rows, k) for k in range(epg)]
    a_in = [pick(aff_rows, k) for k in range(epg)]
    i0, _ = _argmax_first(s_in)
    i1, _ = _argmax_first([jnp.where(i0 == k, -jnp.inf, s_in[k]) for k in range(epg)])

    def take(rows, idx):
        out = rows[0]
        for k in range(1, epg):
            out = jnp.where(idx == k, rows[k], out)
        return out

    w0, w1 = take(a_in, i0), take(a_in, i1)
    tot = w0 + w1
    zf = jnp.zeros((6,) + w0.shape[1:], F32)
    gw_ref[...] = jnp.concatenate([w0 / tot, w1 / tot, zf], axis=0)
    eidx_ref[...] = jnp.concatenate([bg * epg + i0, bg * epg + i1, zf.astype(I32)], axis=0)


def _norm_mod(X, g, mods, shift, scale, tm, midx):
    R, D = X.shape
    return pl.pallas_call(
        functools.partial(_norm_mod_kernel, shift=shift, scale=scale),
        grid=(R // tm,),
        in_specs=[pl.BlockSpec((tm, D), lambda i: (i, 0)),
                  pl.BlockSpec((1, D), lambda i: (0, 0)),
                  pl.BlockSpec((1, N_MOD, D), lambda i: (midx(i), 0, 0))],
        out_specs=pl.BlockSpec((tm, D), lambda i: (i, 0)),
        out_shape=jax.ShapeDtypeStruct((R, D), BF16),
        compiler_params=_cp(("arbitrary",)),
        name="norm_mod",
    )(X, g.reshape(1, D), mods)


def _norm_router(X, g, mods, shift, scale, rw_pad, rb, tm, midx):
    R, D = X.shape
    return pl.pallas_call(
        functools.partial(_norm_router_kernel, shift=shift, scale=scale),
        grid=(R // tm,),
        in_specs=[pl.BlockSpec((tm, D), lambda i: (i, 0)),
                  pl.BlockSpec((1, D), lambda i: (0, 0)),
                  pl.BlockSpec((1, N_MOD, D), lambda i: (midx(i), 0, 0)),
                  pl.BlockSpec((D, LANES), lambda i: (0, 0)),
                  pl.BlockSpec((N_EXPERTS, 1), lambda i: (0, 0))],
        out_specs=[pl.BlockSpec((tm * (D // 2 // LANES), LANES), lambda i: (i, 0)),
                   pl.BlockSpec((8, tm), lambda i: (0, i)),
                   pl.BlockSpec((8, tm), lambda i: (0, i))],
        out_shape=[jax.ShapeDtypeStruct((R * (D // 2 // LANES), LANES), U32),
                   jax.ShapeDtypeStruct((8, R), I32),
                   jax.ShapeDtypeStruct((8, R), F32)],
        compiler_params=_cp(("arbitrary",)),
        name="norm_router",
    )(X, g.reshape(1, D), mods, rw_pad, rb)


def _final_norm_kernel(x_ref, g_ref, o_ref):
    o_ref[...] = _norm_rows(x_ref[...], g_ref[...])


def _final_norm(X, g, n_rows, tm):
    D = X.shape[1]
    return pl.pallas_call(
        _final_norm_kernel,
        grid=(n_rows // tm,),
        in_specs=[pl.BlockSpec((tm, D), lambda i: (i, 0)),
                  pl.BlockSpec((1, D), lambda i: (0, 0))],
        out_specs=pl.BlockSpec((tm, D), lambda i: (i, 0)),
        out_shape=jax.ShapeDtypeStruct((n_rows, D), F32),
        compiler_params=_cp(("arbitrary",)),
        name="final_norm",
    )(X, g.reshape(1, D))


MXU_COLS = 256


def _mm_cast(w_ref, wbf_ref):
    @pl.when(pl.program_id(1) == 0)
    def _():
        wbf_ref[...] = w_ref[...].astype(BF16)


def _mm_sub_blocks(a_ref, wbf_ref):
    tn = wbf_ref.shape[1]
    width = min(tn, MXU_COLS)
    for c0 in range(0, tn, width):
        yield c0, _dot(a_ref[...], wbf_ref[:, c0:c0 + width])


def _mm_bias_kernel(a_ref, w_ref, b_ref, o_ref, wbf_ref):
    _mm_cast(w_ref, wbf_ref)
    for c0, acc in _mm_sub_blocks(a_ref, wbf_ref):
        cols = slice(c0, c0 + acc.shape[1])
        o_ref[:, cols] = (acc + b_ref[:, cols]).astype(o_ref.dtype)


def _mm_res_kernel(a_ref, w_ref, r_ref, mod_ref, o_ref, wbf_ref, *, midx):
    _mm_cast(w_ref, wbf_ref)
    for c0, acc in _mm_sub_blocks(a_ref, wbf_ref):
        cols = slice(c0, c0 + acc.shape[1])
        o_ref[:, cols] = r_ref[:, cols] + mod_ref[0, midx:midx + 1, cols] * acc


def _mm_qkv_kernel(a_ref, w_ref, tab_ref, o_ref, wbf_ref, *, n_rope_tiles, n_q_tiles, q_scale, quarter):
    _mm_cast(w_ref, wbf_ref)
    j = pl.program_id(0)

    def store(fn):
        for c0, acc in _mm_sub_blocks(a_ref, wbf_ref):
            for b in range(acc.shape[1] // LANES):
                o_ref[c0 // LANES + b] = fn(acc[:, b * LANES:(b + 1) * LANES]).astype(BF16)

    if n_rope_tiles:
        @pl.when(j < n_rope_tiles)
        def _():
            c, s1, s2 = tab_ref[0, 0], tab_ref[0, 1], tab_ref[0, 2]
            store(lambda x: x * c + pltpu.roll(x, LANES - quarter, 1) * s1 + pltpu.roll(x, quarter, 1) * s2)

        @pl.when(j >= n_rope_tiles)
        def _():
            store(lambda x: x)
    else:
        @pl.when(j < n_q_tiles)
        def _():
            store(lambda x: x * q_scale)

        @pl.when(j >= n_q_tiles)
        def _():
            store(lambda x: x)


def _mm_bias(a, w3, layer, bias, out_dtype, tn_pref=2048):
    M, K = a.shape
    N = w3.shape[2]
    tn = _tile(N, tn_pref)
    return pl.pallas_call(
        _mm_bias_kernel,
        grid=(N // tn, 1),
        in_specs=[pl.BlockSpec((M, K), lambda j, i: (0, 0)),
                  pl.BlockSpec((None, K, tn), lambda j, i: (layer, 0, j), pipeline_mode=pl.Buffered(1)),
                  pl.BlockSpec((1, tn), lambda j, i: (0, j))],
        out_specs=pl.BlockSpec((M, tn), lambda j, i: (0, j)),
        out_shape=jax.ShapeDtypeStruct((M, N), out_dtype),
        scratch_shapes=[pltpu.VMEM((K, tn), BF16)],
        compiler_params=_cp(("arbitrary", "arbitrary")),
        name="mm_bias",
    )(a, w3, bias)


def _mm_res(a, w3, layer, resid, mods, mod_k, tm, midx, tn_pref=1024):
    M, K = a.shape
    N = w3.shape[2]
    tn = _tile(N, tn_pref)
    return pl.pallas_call(
        functools.partial(_mm_res_kernel, midx=mod_k),
        grid=(N // tn, M // tm),
        in_specs=[pl.BlockSpec((tm, K), lambda j, i: (i, 0)),
                  pl.BlockSpec((None, K, tn), lambda j, i: (layer, 0, j), pipeline_mode=pl.Buffered(1)),
                  pl.BlockSpec((tm, tn), lambda j, i: (i, j)),
                  pl.BlockSpec((1, N_MOD, tn), lambda j, i: (midx(i), 0, j))],
        out_specs=pl.BlockSpec((tm, tn), lambda j, i: (i, j)),
        out_shape=jax.ShapeDtypeStruct((M, N), F32),
        scratch_shapes=[pltpu.VMEM((K, tn), BF16)],
        input_output_aliases={2: 0},
        compiler_params=_cp(("arbitrary", "arbitrary")),
        name="mm_res",
    )(a, w3, resid, mods)


def _mm_qkv(a, w3, layer, tabs, tm, tn, n_q_tiles, n_rope_tiles, q_scale, quarter):
    M, K = a.shape
    N = w3.shape[2]
    nblk = tn // LANES
    if tabs is None:
        tabs = jnp.zeros((1, 3, 8, LANES), F32)
        tab_spec = pl.BlockSpec((1, 3, 8, LANES), lambda j, i: (0, 0, 0, 0))
    else:
        tab_spec = pl.BlockSpec((1, 3, tm, LANES), lambda j, i: (jnp.minimum(j // n_q_tiles, 1), 0, i, 0))
    return pl.pallas_call(
        functools.partial(_mm_qkv_kernel, n_rope_tiles=n_rope_tiles, n_q_tiles=n_q_tiles,
                          q_scale=q_scale, quarter=quarter),
        grid=(N // tn, M // tm),
        in_specs=[pl.BlockSpec((tm, K), lambda j, i: (i, 0)),
                  pl.BlockSpec((None, K, tn), lambda j, i: (layer, 0, j), pipeline_mode=pl.Buffered(1)),
                  tab_spec],
        out_specs=pl.BlockSpec((nblk, tm, LANES), lambda j, i: (j, i, 0)),
        out_shape=jax.ShapeDtypeStruct((N // LANES, M, LANES), BF16),
        scratch_shapes=[pltpu.VMEM((K, tn), BF16)],
        compiler_params=_cp(("arbitrary", "arbitrary")),
        name="mm_qkv",
    )(a, w3, tabs)


def _rope_tables(n_lat, n_batch, n_ctx_rows, head_dim, q_scale):
    half, quarter = head_dim // 2, head_dim // 4
    pos = np.arange(n_lat)
    row, col = (pos // GRID_W).astype(np.float32), (pos % GRID_W).astype(np.float32)
    lane = np.arange(LANES) % head_dim
    inv = jnp.asarray(ROPE_THETA, F32) ** (-jnp.arange(quarter, dtype=F32) / quarter)
    p = jnp.where(jnp.asarray(lane // half == 0)[None, :], jnp.asarray(row)[:, None], jnp.asarray(col)[:, None])
    ang = p * inv[np.asarray((lane % half) % quarter)][None, :]
    first = jnp.asarray((lane % half) < quarter)[None, :]
    cos, sin = jnp.cos(ang), jnp.sin(ang)
    t = jnp.stack([cos, jnp.where(first, -sin, 0.0), jnp.where(first, 0.0, sin)])
    t = jnp.tile(t, (1, n_batch, 1))
    ident = jnp.stack([jnp.ones((n_ctx_rows, LANES), F32), jnp.zeros((n_ctx_rows, LANES), F32),
                       jnp.zeros((n_ctx_rows, LANES), F32)])
    t = jnp.concatenate([t, ident], axis=1)
    return jnp.stack([t * q_scale, t])


def _lane_fold(x, op, acc):
    for j in range(x.shape[1] // LANES):
        acc = op(acc, x[:, j * LANES:(j + 1) * LANES])
    return acc


def _da_kernel(lam_ref, q_ref, kc_ref, vc_ref, *rest, has_lat, out_scale, kchunk):
    if has_lat:
        kl_ref, vl_ref, g_ref, o_ref, s0_scr, s1_scr = rest
    else:
        g_ref, _, o_ref, s0_scr, s1_scr = rest
    s_scr = (s0_scr, s1_scr)
    lam = lam_ref[0]
    tq = q_ref.shape[1]
    C = kc_ref.shape[1]
    chunks = [(kc_ref, vc_ref, 0, C, 0)]
    if has_lat:
        chunks += [(kl_ref, vl_ref, j * kchunk, kchunk, C + j * kchunk) for j in range(kl_ref.shape[1] // kchunk)]

    rsub = math.gcd(tq, DA_STATE_ROWS)
    row_blocks = [slice(r, r + rsub) for r in range(0, tq, rsub)]

    def score_chunk(m, chunk, mrun):
        k_ref, _, off, n, col = chunk
        s = _dot_t(q_ref[m], k_ref[m, off:off + n, :])
        s_scr[m][:, col:col + n] = s
        return [_lane_fold(s[rows], jnp.maximum, mr) for rows, mr in zip(row_blocks, mrun)]

    def prob_chunk(m, chunk, mx, state):
        _, v_ref, off, n, col = chunk
        lrun, o = state
        p, lnew = [], []
        for rows, mxr, lr in zip(row_blocks, mx, lrun):
            pr = jnp.exp2(s_scr[m][rows, col:col + n] - mxr)
            lnew.append(_lane_fold(pr, jnp.add, lr))
            p.append(pr.astype(BF16))
        v = jnp.concatenate([v_ref[0, off:off + n, :], v_ref[1, off:off + n, :]], axis=1)
        return lnew, o + _dot(jnp.concatenate(p, axis=0), v)

    def row_max(mrun):
        return [jnp.max(mr, axis=-1, keepdims=True) for mr in mrun]

    def row_sum(lrun):
        return jnp.concatenate([jnp.sum(lr, axis=-1, keepdims=True) for lr in lrun], axis=0)

    m_init = [jnp.full((rsub, LANES), -jnp.inf, F32) for _ in row_blocks]
    st_init = ([jnp.zeros((rsub, LANES), F32) for _ in row_blocks], jnp.zeros((tq, 2 * LANES), F32))
    mrun0 = m_init
    for ch in chunks:
        mrun0 = score_chunk(0, ch, mrun0)
    mx0 = row_max(mrun0)
    mrun1, st0 = m_init, st_init
    for ch in chunks:
        mrun1 = score_chunk(1, ch, mrun1)
        st0 = prob_chunk(0, ch, mx0, st0)
    mx1 = row_max(mrun1)
    st1 = st_init
    for ch in chunks:
        st1 = prob_chunk(1, ch, mx1, st1)
    o = st0[1] * (1.0 / row_sum(st0[0])) - st1[1] * (lam / row_sum(st1[0]))
    o_ref[...] = (_norm_rows(o, g_ref[...]) * out_scale).astype(BF16)


def _da_attention(qkv, lam, subln_g, out_scale, n_batch, S, C, H, tq):
    R = qkv.shape[1]
    n_lat = n_batch * S
    d2 = 2 * DA_HEAD_DIM
    g = subln_g.reshape(1, d2)
    smem = pl.BlockSpec(memory_space=pltpu.SMEM)
    tql = _tile(S, tq, 8)
    kchunk = _tile(S, 512)
    out = pl.pallas_call(
        functools.partial(_da_kernel, has_lat=True, out_scale=out_scale, kchunk=kchunk),
        grid=(n_batch, H, S // tql),
        in_specs=[smem,
                  pl.BlockSpec((2, tql, LANES), lambda b, h, i: (h, b * (S // tql) + i, 0)),
                  pl.BlockSpec((2, C, LANES), lambda b, h, i: (H + h, n_lat // C + b, 0)),
                  pl.BlockSpec((2, C, LANES), lambda b, h, i: (2 * H + h, n_lat // C + b, 0)),
                  pl.BlockSpec((2, S, LANES), lambda b, h, i: (H + h, b, 0)),
                  pl.BlockSpec((2, S, LANES), lambda b, h, i: (2 * H + h, b, 0)),
                  pl.BlockSpec((1, d2), lambda b, h, i: (0, 0))],
        out_specs=pl.BlockSpec((tql, d2), lambda b, h, i: (b * (S // tql) + i, h)),
        out_shape=jax.ShapeDtypeStruct((R, H * d2), BF16),
        scratch_shapes=[pltpu.VMEM((tql, C + S), F32), pltpu.VMEM((tql, C + S), F32)],
        compiler_params=_cp(("arbitrary", "arbitrary", "arbitrary")),
        name="da_attn_lat",
    )(lam, qkv, qkv, qkv, qkv, qkv, g)
    tqc = _tile(C, tq, 8)
    return pl.pallas_call(
        functools.partial(_da_kernel, has_lat=False, out_scale=out_scale, kchunk=kchunk),
        grid=(n_batch, H, C // tqc),
        in_specs=[smem,
                  pl.BlockSpec((2, tqc, LANES), lambda b, h, i: (h, (n_lat + b * C) // tqc + i, 0)),
                  pl.BlockSpec((2, C, LANES), lambda b, h, i: (H + h, n_lat // C + b, 0)),
                  pl.BlockSpec((2, C, LANES), lambda b, h, i: (2 * H + h, n_lat // C + b, 0)),
                  pl.BlockSpec((1, d2), lambda b, h, i: (0, 0)),
                  pl.BlockSpec(memory_space=pl.ANY)],
        out_specs=pl.BlockSpec((tqc, d2), lambda b, h, i: ((n_lat + b * C) // tqc + i, h)),
        out_shape=jax.ShapeDtypeStruct((R, H * d2), BF16),
        scratch_shapes=[pltpu.VMEM((tqc, C), F32), pltpu.VMEM((tqc, C), F32)],
        input_output_aliases={5: 0},
        compiler_params=_cp(("arbitrary", "arbitrary", "arbitrary")),
        name="da_attn_ctx",
    )(lam, qkv, qkv, qkv, g, out)


def _na_geometry(rows):
    kh, rq = min(NA_KH, rows), NA_ROWS_PER_TILE
    kr = kh + rq
    assert rows % rq == 0 and rows >= kr and rows // rq >= 3
    return kh, rq, kr


def _na_bias(rpb, rows):
    kh, rq, kr = _na_geometry(rows)
    W, kw = GRID_W, NA_KW
    n_tiles = rows // rq
    H, n_dr = rpb.shape[0], rpb.shape[1]
    assert LANES % W == 0 and (kr * W) % LANES == 0

    def row_geometry(i):
        r = i * rq + np.arange(rq)
        k = int(np.clip(i * rq - kh // 2, 0, rows - kr)) + np.arange(kr)
        rs = np.clip(r - kh // 2, 0, rows - kh)
        valid = (k[None, :] >= rs[:, None]) & (k[None, :] < rs[:, None] + kh)
        return np.where(valid, k[None, :] - r[:, None] + (NA_KH - 1), n_dr)

    pats = [row_geometry(0), row_geometry(1), row_geometry(n_tiles - 1)]
    for i in range(1, n_tiles - 1):
        assert (row_geometry(i) == pats[1]).all()
    idx = jnp.asarray(np.stack(pats).reshape(-1), I32)

    L = 2 * W - 1
    u = jnp.pad(rpb.astype(F32), ((0, 0), (0, 0), (0, L - rpb.shape[2])))
    t = jnp.tile(u, (1, 1, W))[:, :, :W * (L - 1)].reshape(H, rpb.shape[1], W, L - 1)[..., kw - 1:kw - 1 + W]
    c = np.arange(W)
    cs = np.clip(c - kw // 2, 0, W - kw)
    vcol = (c[None, :] >= cs[:, None]) & (c[None, :] < cs[:, None] + kw)
    t = jnp.where(vcol[None, None], t * LOG2E, NEG_BIG)
    t = jnp.concatenate([t, jnp.full((H, 1, W, W), NEG_BIG, F32)], axis=1)
    per = LANES // W
    tab = jnp.stack([jnp.pad(t, ((0, 0), (0, 0), (0, 0), (s * W, LANES - (s + 1) * W))) for s in range(per)])
    return tab, idx


def _na_kernel(q_ref, kc_ref, vc_ref, *rest, has_loc, rows):
    if has_loc:
        k_ref, v_ref, idx_ref, tab_ref, o_ref = rest
        kh, rq, kr = _na_geometry(rows)
        i = pl.program_id(2)
        n_tiles = rows // rq
        start = jnp.clip(i * rq - kh // 2, 0, rows - kr) * GRID_W
        start = pl.multiple_of(start, GRID_W)
        pat = jnp.where(i == 0, 0, jnp.where(i == n_tiles - 1, 2, 1))
        per = tab_ref.shape[0]

        def bias_of(hb):
            rows_out = []
            for qr in range(rq):
                blocks = []
                for j in range(kr // per):
                    piece = None
                    for s in range(per):
                        dr = idx_ref[(pat * rq + qr) * kr + j * per + s]
                        term = tab_ref[s, hb, pl.ds(dr, 1)][0]
                        piece = term if piece is None else piece + term
                    blocks.append(piece)
                rows_out.append(jnp.concatenate(blocks, axis=1))
            return jnp.concatenate(rows_out, axis=0)
    else:
        _, o_ref = rest
    for hb in range(q_ref.shape[0]):
        q = q_ref[hb]
        sc = _dot_t(q, kc_ref[hb])
        mx = jnp.max(sc, axis=-1, keepdims=True)
        if has_loc:
            sl = _dot_t(q, k_ref[hb, pl.ds(start, kr * GRID_W), :]) + bias_of(hb)
            mx = jnp.maximum(mx, jnp.max(sl, axis=-1, keepdims=True))
        pc = jnp.exp2(sc - mx)
        l = jnp.sum(pc, axis=-1, keepdims=True)
        o = _dot(pc.astype(BF16), vc_ref[hb])
        if has_loc:
            pl_ = jnp.exp2(sl - mx)
            l = l + jnp.sum(pl_, axis=-1, keepdims=True)
            o = o + _dot(pl_.astype(BF16), v_ref[hb, pl.ds(start, kr * GRID_W), :])
        o_ref[:, hb * LANES:(hb + 1) * LANES] = (o * (1.0 / l)).astype(BF16)


def _na_attention(qkv, bias, n_batch, S, C, H):
    R = qkv.shape[1]
    n_lat = n_batch * S
    rows = S // GRID_W
    kh, rq, kr = _na_geometry(rows)
    tq, n_tiles = rq * GRID_W, rows // rq
    d = NA_HEAD_DIM
    tab, idx = bias
    hb = math.gcd(H, NA_HEADS_PER_STEP)
    nhb = H // hb
    out = pl.pallas_call(
        functools.partial(_na_kernel, has_loc=True, rows=rows),
        grid=(n_batch, nhb, n_tiles),
        in_specs=[pl.BlockSpec((hb, tq, LANES), lambda b, h, i: (h, b * n_tiles + i, 0)),
                  pl.BlockSpec((hb, C, LANES), lambda b, h, i: (nhb + h, n_lat // C + b, 0)),
                  pl.BlockSpec((hb, C, LANES), lambda b, h, i: (2 * nhb + h, n_lat // C + b, 0)),
                  pl.BlockSpec((hb, S, LANES), lambda b, h, i: (nhb + h, b, 0)),
                  pl.BlockSpec((hb, S, LANES), lambda b, h, i: (2 * nhb + h, b, 0)),
                  pl.BlockSpec(memory_space=pltpu.SMEM),
                  pl.BlockSpec((tab.shape[0], hb) + tab.shape[2:], lambda b, h, i: (0, h, 0, 0, 0))],
        out_specs=pl.BlockSpec((tq, hb * d), lambda b, h, i: (b * n_tiles + i, h)),
        out_shape=jax.ShapeDtypeStruct((R, H * d), BF16),
        compiler_params=_cp(("arbitrary", "arbitrary", "arbitrary")),
        name="na_attn_lat",
    )(qkv, qkv, qkv, qkv, qkv, idx, tab)
    return pl.pallas_call(
        functools.partial(_na_kernel, has_loc=False, rows=rows),
        grid=(n_batch, nhb, 1),
        in_specs=[pl.BlockSpec((hb, C, LANES), lambda b, h, i: (h, n_lat // C + b, 0)),
                  pl.BlockSpec((hb, C, LANES), lambda b, h, i: (nhb + h, n_lat // C + b, 0)),
                  pl.BlockSpec((hb, C, LANES), lambda b, h, i: (2 * nhb + h, n_lat // C + b, 0)),
                  pl.BlockSpec(memory_space=pl.ANY)],
        out_specs=pl.BlockSpec((C, hb * d), lambda b, h, i: (n_lat // C + b, h)),
        out_shape=jax.ShapeDtypeStruct((R, H * d), BF16),
        input_output_aliases={3: 0},
        compiler_params=_cp(("arbitrary", "arbitrary", "arbitrary")),
        name="na_attn_ctx",
    )(qkv, qkv, qkv, out)


def _sw_kernel(sink_ref, q_ref, kc_ref, vc_ref, *rest, has_loc, tq, band, S, G):
    if has_loc:
        k_ref, v_ref, o_ref = rest
    else:
        _, o_ref = rest
    hd = SW_HEAD_DIM
    per_blk = LANES // hd
    g2n = G // per_blk
    C = kc_ref.shape[1]
    c, i = pl.program_id(1), pl.program_id(2)
    lane = lax.broadcasted_iota(I32, (1, LANES), 1)

    def block_diag(x, par):
        sw = jnp.concatenate([x[:, hd:], x[:, :hd]], axis=1)
        zero = jnp.zeros_like(x)
        top = jnp.where(lane < hd, x if par == 0 else sw, zero)
        bot = jnp.where(lane >= hd, sw if par == 0 else x, zero)
        return jnp.concatenate([top, bot], axis=0)

    if has_loc:
        bstart = pl.multiple_of(jnp.clip(i * tq - SW_WINDOW, 0, S - band), LANES)
        kb = k_ref[0, pl.ds(bstart, band), :]
        vb = v_ref[0, pl.ds(bstart, band), :]
        qpos = i * tq + lax.broadcasted_iota(I32, (tq, 1), 0)
        kpos = bstart + lax.broadcasted_iota(I32, (1, band), 1)
        valid = jnp.abs(kpos - qpos) <= SW_WINDOW

    for par in range(2):
        kc_bd, vc_bd = block_diag(kc_ref[0], par), block_diag(vc_ref[0], par)
        if has_loc:
            kl_bd, vl_bd = block_diag(kb, par), block_diag(vb, par)
        for g2 in range(g2n):
            blk = par * g2n + g2
            q = q_ref[blk]
            sc2 = _dot_t(q, kc_bd)
            if has_loc:
                sl2 = _dot_t(q, kl_bd)
            pcs, pls, rls = [], [], []
            for hh in range(per_blk):
                sink = sink_ref[(2 * c + par) * G + g2 * per_blk + hh] * LOG2E
                sc = sc2[:, hh * C:(hh + 1) * C]
                mx = jnp.maximum(jnp.max(sc, axis=-1, keepdims=True), sink)
                if has_loc:
                    sl = jnp.where(valid, sl2[:, hh * band:(hh + 1) * band], NEG_BIG)
                    mx = jnp.maximum(mx, jnp.max(sl, axis=-1, keepdims=True))
                pc = jnp.exp2(sc - mx)
                l = jnp.sum(pc, axis=-1, keepdims=True) + jnp.exp2(sink - mx)
                pcs.append(pc)
                if has_loc:
                    pl_ = jnp.exp2(sl - mx)
                    l = l + jnp.sum(pl_, axis=-1, keepdims=True)
                    pls.append(pl_)
                rls.append(1.0 / l)
            o = _dot(jnp.concatenate(pcs, axis=1).astype(BF16), vc_bd)
            if has_loc:
                o = o + _dot(jnp.concatenate(pls, axis=1).astype(BF16), vl_bd)
            o = o * jnp.where(lane < hd, rls[0], rls[1])
            o_ref[:, blk * LANES:(blk + 1) * LANES] = o.astype(BF16)


def _sw_attention(qkv, sink, n_batch, S, C, H, tq):
    R = qkv.shape[1]
    n_lat = n_batch * S
    hd, KV = SW_HEAD_DIM, SW_KV_HEADS
    G = H // KV
    nqb, nkb = H * hd // LANES, KV * hd // LANES
    qpk = 2 * G * hd // LANES
    assert KV % 2 == 0 and LANES // hd == 2 and G % 2 == 0
    tq = _tile(S, tq)
    band = tq + 2 * SW_WINDOW
    assert band <= S and SW_WINDOW % LANES == 0
    smem = pl.BlockSpec(memory_space=pltpu.SMEM)
    out = pl.pallas_call(
        functools.partial(_sw_kernel, has_loc=True, tq=tq, band=band, S=S, G=G),
        grid=(n_batch, KV // 2, S // tq),
        in_specs=[smem,
                  pl.BlockSpec((qpk, tq, LANES), lambda b, c, i: (c, b * (S // tq) + i, 0)),
                  pl.BlockSpec((1, C, LANES), lambda b, c, i: (nqb + c, n_lat // C + b, 0)),
                  pl.BlockSpec((1, C, LANES), lambda b, c, i: (nqb + nkb + c, n_lat // C + b, 0)),
                  pl.BlockSpec((1, S, LANES), lambda b, c, i: (nqb + c, b, 0)),
                  pl.BlockSpec((1, S, LANES), lambda b, c, i: (nqb + nkb + c, b, 0))],
        out_specs=pl.BlockSpec((tq, qpk * LANES), lambda b, c, i: (b * (S // tq) + i, c)),
        out_shape=jax.ShapeDtypeStruct((R, H * hd), BF16),
        compiler_params=_cp(("arbitrary", "arbitrary", "arbitrary")),
        name="sw_attn_lat",
    )(sink, qkv, qkv, qkv, qkv, qkv)
    return pl.pallas_call(
        functools.partial(_sw_kernel, has_loc=False, tq=C, band=0, S=S, G=G),
        grid=(n_batch, KV // 2, 1),
        in_specs=[smem,
                  pl.BlockSpec((qpk, C, LANES), lambda b, c, i: (c, n_lat // C + b, 0)),
                  pl.BlockSpec((1, C, LANES), lambda b, c, i: (nqb + c, n_lat // C + b, 0)),
                  pl.BlockSpec((1, C, LANES), lambda b, c, i: (nqb + nkb + c, n_lat // C + b, 0)),
                  pl.BlockSpec(memory_space=pl.ANY)],
        out_specs=pl.BlockSpec((C, qpk * LANES), lambda b, c, i: (n_lat // C + b, c)),
        out_shape=jax.ShapeDtypeStruct((R, H * hd), BF16),
        input_output_aliases={4: 0},
        compiler_params=_cp(("arbitrary", "arbitrary", "arbitrary")),
        name="sw_attn_ctx",
    )(sink, qkv, qkv, qkv, out)


def _moe_plan(eidx, tme):
    e_flat = eidx[:2].reshape(-1)
    onehot = (e_flat[:, None] == jnp.arange(N_EXPERTS, dtype=I32)[None, :]).astype(I32)
    csum = jnp.cumsum(onehot, axis=0)
    rank = jnp.sum(csum * onehot, axis=1) - 1
    cnt = csum[-1]
    pcnt = (cnt + tme - 1) // tme * tme
    ends = jnp.cumsum(pcnt)
    offs = ends - pcnt
    dest = (offs[e_flat] + rank).astype(I32)
    pad_row = jnp.where(cnt % tme != 0, ends - tme, -1).astype(I32)
    return dest, (pcnt // tme).astype(I32), (offs // tme).astype(I32), pad_row


def _work_list(tiles_e, tile_base, n_chunks, n_items_max):
    items_e = tiles_e * n_chunks
    ends = jnp.cumsum(items_e)
    nw = ends[-1]
    w = jnp.minimum(jnp.arange(n_items_max, dtype=I32), nw - 1)
    e = jnp.sum((w[:, None] >= ends[None, :]).astype(I32), axis=1)
    local = w - (ends[e] - items_e[e])
    te = jnp.maximum(tiles_e[e], 1)
    chunk, r = local // te, local % te
    return (e.astype(I32), chunk.astype(I32), (tile_base[e] + r).astype(I32),
            (r == 0).astype(I32), nw.astype(I32).reshape(1))


def _token_copy(src, src_tok, dst, dst_tok, k, sem):
    def first_row(tok):
        return tok * k if isinstance(tok, int) else pl.multiple_of(tok * k, k)

    return pltpu.make_async_copy(src.at[pl.ds(first_row(src_tok), k)], dst.at[pl.ds(first_row(dst_tok), k)], sem)


def _scatter_rows_kernel(dest_ref, pad_ref, hp_ref, o_hbm, zbuf, sem, zsem, *, ts, n_rows, tme, k):
    def zero_copy(e):
        row = pl.multiple_of(pad_ref[e] * k, 8)
        return pltpu.make_async_copy(zbuf, o_hbm.at[pl.ds(row, tme * k)], zsem)

    @pl.when(pl.program_id(0) == 0)
    def _():
        zbuf[...] = jnp.zeros_like(zbuf)
        for e in range(N_EXPERTS):
            pl.when(pad_ref[e] >= 0)(lambda e=e: zero_copy(e).start())
        for e in range(N_EXPERTS):
            pl.when(pad_ref[e] >= 0)(lambda e=e: zero_copy(e).wait())

    base = pl.program_id(0) * ts

    def issue(r, carry):
        for slot in range(2):
            _token_copy(hp_ref, r, o_hbm, dest_ref[slot * n_rows + base + r], k, sem).start()
        return carry

    def drain(r, carry):
        for slot in range(2):
            _token_copy(hp_ref, r, o_hbm, 0, k, sem).wait()
        return carry

    lax.fori_loop(0, ts, issue, 0)
    lax.fori_loop(0, ts, drain, 0)


def _scatter_rows(hp, dest, pad_row, n_rows, tme, p_max):
    k = hp.shape[0] // n_rows
    ts = _tile(n_rows, 256, 8)
    return pl.pallas_call(
        functools.partial(_scatter_rows_kernel, ts=ts, n_rows=n_rows, tme=tme, k=k),
        grid_spec=pltpu.PrefetchScalarGridSpec(
            num_scalar_prefetch=2, grid=(n_rows // ts,),
            in_specs=[pl.BlockSpec((ts * k, LANES), lambda i, d, p: (i, 0))],
            out_specs=pl.BlockSpec(memory_space=pl.ANY),
            scratch_shapes=[pltpu.VMEM((tme * k, LANES), hp.dtype), pltpu.SemaphoreType.DMA(()),
                            pltpu.SemaphoreType.DMA(())]),
        out_shape=jax.ShapeDtypeStruct((p_max * k, LANES), hp.dtype),
        compiler_params=_cp(("arbitrary",)),
        name="moe_scatter",
    )(dest, pad_row, hp)


def _silu(g):
    return g / (1.0 + jnp.exp(-g))


def _moe_up_kernel(ie, inext, it, ifirst, nw, hp_ref, wg_hbm, wu_hbm, a_ref, wg32, wu32, wgb, wub, sem, *, layer):
    w = pl.program_id(0)

    def fetch(e):
        return (pltpu.make_async_copy(wg_hbm.at[layer, e], wg32, sem.at[0]),
                pltpu.make_async_copy(wu_hbm.at[layer, e], wu32, sem.at[1]))

    @pl.when(w == 0)
    def _():
        for copy in fetch(ie[0]):
            copy.start()

    @pl.when(w < nw[0])
    def _():
        @pl.when(ifirst[w] == 1)
        def _():
            for copy in fetch(ie[w]):
                copy.wait()
            wgb[...] = wg32[...].astype(BF16)
            wub[...] = wu32[...].astype(BF16)

            @pl.when(inext[w] >= 0)
            def _():
                for copy in fetch(inext[w]):
                    copy.start()

        hp = jnp.concatenate(_load_token_major(hp_ref, a_ref.shape[0]), axis=1)
        half = hp.shape[1]
        lo, hi = (v.astype(BF16) for v in _unpack_bf16_pair(hp))
        g = _dot(lo, wgb[:half]) + _dot(hi, wgb[half:])
        u = _dot(lo, wub[:half]) + _dot(hi, wub[half:])
        a_ref[...] = (_silu(g) * u).astype(BF16)


def _moe_down_kernel(ie, ic, it, ifirst, nw, a_ref, wd_ref, y_ref, wdb):
    w = pl.program_id(0)

    @pl.when(w < nw[0])
    def _():
        @pl.when(ifirst[w] == 1)
        def _():
            wdb[...] = wd_ref[...].astype(BF16)

        half = y_ref.shape[1]
        width = min(half, MXU_COLS)
        a = a_ref[...]
        for c0 in range(0, half, width):
            lo = _dot(a, wdb[:, c0:c0 + width])
            hi = _dot(a, wdb[:, half + c0:half + c0 + width])
            y_ref[:, c0:c0 + width] = _pack_bf16_pair(lo, hi)


def _pack_bf16_pair(lo, hi):
    lo_bits = pltpu.bitcast(lo.astype(BF16).astype(F32), U32)
    hi_bits = pltpu.bitcast(hi.astype(BF16).astype(F32), U32)
    return (lo_bits >> 16) | (hi_bits & jnp.uint32(0xFFFF0000))


def _unpack_bf16_pair(p):
    return pltpu.bitcast(p << 16, F32), pltpu.bitcast(p & jnp.uint32(0xFFFF0000), F32)


def _row_copy(src_hbm, src_row, dst, dst_row, sem):
    return pltpu.make_async_copy(src_hbm.at[pl.ds(src_row, 1)], dst.at[pl.ds(dst_row, 1)], sem)


def _combine_kernel(dest_ref, y_hbm, x_ref, mod_ref, gate_ref, *rest, tc, n_rows, mod_k, next_norm):
    if next_norm:
        g_next_ref, mod_next_ref, o_ref, h_ref, buf, sem = rest
    else:
        o_ref, buf, sem = rest
    i, n_steps = pl.program_id(0), pl.num_programs(0)

    def gather(step, start):
        tile_buf, tile_sem = buf.at[step % 2], sem.at[step % 2]

        def body(r, carry):
            for slot in range(2):
                src_row = dest_ref[slot * n_rows + step * tc + r] if start else 0
                copy = _row_copy(y_hbm, src_row, tile_buf.at[slot], r, tile_sem)
                copy.start() if start else copy.wait()
            return carry

        lax.fori_loop(0, tc, body, 0)

    pl.when(i == 0)(lambda: gather(i, True))
    pl.when(i + 1 < n_steps)(lambda: gather(i + 1, True))
    gather(i, False)
    cur = buf.at[i % 2]
    half = buf.shape[3]
    g0, g1 = gate_ref[:, 0:1], gate_ref[:, 1:2]
    width = min(half, 4 * LANES)
    for c0 in range(0, half, width):
        cols = slice(c0, c0 + width)
        y0_lo, y0_hi = _unpack_bf16_pair(cur[0, :, cols])
        y1_lo, y1_hi = _unpack_bf16_pair(cur[1, :, cols])
        for off, y in ((0, g0 * y0_lo + g1 * y1_lo), (half, g0 * y0_hi + g1 * y1_hi)):
            out_cols = slice(off + c0, off + c0 + width)
            o_ref[:, out_cols] = x_ref[:, out_cols] + mod_ref[0, mod_k:mod_k + 1, out_cols] * y
    if next_norm:
        shift, scale = next_norm
        h = _norm_rows(o_ref[...], g_next_ref[...])
        h_ref[...] = (h * (1.0 + mod_next_ref[0, scale:scale + 1, :]) + mod_next_ref[0, shift:shift + 1, :]).astype(BF16)


def _moe(X, hp, eidx, gw, w_gate, w_up, w_down, layer, mods, mod_k, midx, next_norm=None):
    R, D = X.shape
    d_ff = w_gate.shape[3]
    tme = MOE_ROW_TILE
    p_max = -(-(2 * R + N_EXPERTS * (tme - 1)) // tme) * tme
    t_max = p_max // tme
    dest, tiles_e, tile_base, pad_row = _moe_plan(eidx, tme)
    hs = _scatter_rows(hp, dest, pad_row, R, tme, p_max)
    kp = D // 2 // LANES

    items = _work_list(tiles_e, tile_base, 1, t_max)
    e_ids = jnp.arange(N_EXPERTS, dtype=I32)
    routed_at_or_after = lax.cummin(jnp.where(tiles_e > 0, e_ids, N_EXPERTS)[::-1])[::-1]
    next_routed = jnp.concatenate([routed_at_or_after[1:], jnp.full((1,), N_EXPERTS, I32)])
    next_routed = jnp.where(next_routed < N_EXPERTS, next_routed, -1).astype(I32)
    ie, _, it_, ifirst, nw = items
    a = pl.pallas_call(
        functools.partial(_moe_up_kernel, layer=layer),
        grid_spec=pltpu.PrefetchScalarGridSpec(
            num_scalar_prefetch=5, grid=(t_max,),
            in_specs=[pl.BlockSpec((tme * kp, LANES), lambda w, ie, nx, it, fi, nw: (it[w], 0)),
                      pl.BlockSpec(memory_space=pl.ANY), pl.BlockSpec(memory_space=pl.ANY)],
            out_specs=pl.BlockSpec((tme, d_ff), lambda w, ie, nx, it, fi, nw: (it[w], 0)),
            scratch_shapes=[pltpu.VMEM((D, d_ff), F32), pltpu.VMEM((D, d_ff), F32),
                            pltpu.VMEM((D, d_ff), BF16), pltpu.VMEM((D, d_ff), BF16),
                            pltpu.SemaphoreType.DMA((2,))]),
        out_shape=jax.ShapeDtypeStruct((p_max, d_ff), BF16),
        compiler_params=_cp(("arbitrary",)),
        name="moe_up",
    )(ie, next_routed[ie], it_, ifirst, nw, hs, w_gate, w_up)

    y = pl.pallas_call(
        _moe_down_kernel,
        grid_spec=pltpu.PrefetchScalarGridSpec(
            num_scalar_prefetch=5, grid=(t_max,),
            in_specs=[pl.BlockSpec((tme, d_ff), lambda w, ie, ic, it, fi, nw: (it[w], 0)),
                      pl.BlockSpec((None, None, d_ff, D), lambda w, ie, ic, it, fi, nw: (layer, ie[w], 0, 0))],
            out_specs=pl.BlockSpec((tme, D // 2), lambda w, ie, ic, it, fi, nw: (it[w], 0)),
            scratch_shapes=[pltpu.VMEM((d_ff, D), BF16)]),
        out_shape=jax.ShapeDtypeStruct((p_max, D // 2), U32),
        compiler_params=_cp(("arbitrary",)),
        name="moe_down",
    )(*items, a, w_down)

    tc = _tile(R, 128, 8)
    mod_of_tile = midx(tc)
    row_spec = pl.BlockSpec((tc, D), lambda i, d: (i, 0))
    mod_spec = pl.BlockSpec((1, N_MOD, D), lambda i, d: (mod_of_tile(i), 0, 0))
    in_specs = [pl.BlockSpec(memory_space=pl.ANY), row_spec, mod_spec, pl.BlockSpec((tc, 2), lambda i, d: (i, 0))]
    operands = [dest, y, X, mods, gw[:2].T]
    out_specs, out_shape = row_spec, jax.ShapeDtypeStruct((R, D), F32)
    if next_norm is not None:
        g_next, mods_next, shift, scale = next_norm
        in_specs += [pl.BlockSpec((1, D), lambda i, d: (0, 0)), mod_spec]
        operands += [g_next.reshape(1, D), mods_next]
        out_specs, out_shape = [row_spec, row_spec], [out_shape, jax.ShapeDtypeStruct((R, D), BF16)]
    return pl.pallas_call(
        functools.partial(_combine_kernel, tc=tc, n_rows=R, mod_k=mod_k,
                          next_norm=None if next_norm is None else (shift, scale)),
        grid_spec=pltpu.PrefetchScalarGridSpec(
            num_scalar_prefetch=1, grid=(R // tc,),
            in_specs=in_specs,
            out_specs=out_specs,
            scratch_shapes=[pltpu.VMEM((2, 2, tc, D // 2), U32), pltpu.SemaphoreType.DMA((2,))]),
        out_shape=out_shape,
        input_output_aliases={2: 0},
        compiler_params=_cp(("arbitrary",)),
        name="moe_combine",
    )(*operands)


def kernel(x, c, ctx, c_ctx, cond_down, mod_w, mod_b, norm_g, final_norm_g, da_w_qkv, da_w_o, da_lambda, da_subln_g, na_w_qkv, na_w_o, na_rpb, sw_w_qkv, sw_w_o, sw_sink, router_w, router_bias, moe_w_gate, moe_w_up, moe_w_down):
    B, S, D = x.shape
    C = ctx.shape[1]
    depth = mod_w.shape[0]
    n_lat, n_ctx = B * S, B * C
    R = n_lat + n_ctx

    def midx(tm):
        assert S % tm == 0 and n_ctx % tm == 0
        return _mod_index_fn(n_lat // tm, S // tm, B)

    X = jnp.concatenate([x.reshape(n_lat, D), ctx.reshape(n_ctx, D)], axis=0)

    cond_rows = 16
    cin = jnp.concatenate([c, c_ctx[None, :], jnp.zeros((cond_rows - B - 1, D), F32)], axis=0)
    cin = jax.nn.silu(cin).astype(BF16)
    cond = _mm_bias(cin, cond_down[None], 0, jnp.zeros((1, cond_down.shape[1]), F32), BF16)

    rw_pad = jnp.zeros((D, LANES), F32).at[:, :N_EXPERTS].set(router_w).astype(BF16)
    rb = router_bias.astype(F32).reshape(N_EXPERTS, 1)

    tm_row = _tile(math.gcd(S, n_ctx), 256, 8)
    tm_mm = _tile(math.gcd(S, n_ctx), 512, 8)
    da_h = da_w_qkv.shape[2] // (6 * DA_HEAD_DIM)
    na_h = na_w_qkv.shape[2] // (3 * NA_HEAD_DIM)
    sw_h = sw_sink.shape[1]
    tabs = {}

    all_mods = [_mm_bias(cond, mod_w, i, mod_b[i][None, :], F32)[:B + 1].reshape(B + 1, N_MOD, D)
                for i in range(depth)]
    h = _norm_mod(X, norm_g[0, 0], all_mods[0], 0, 1, tm_row, midx(tm_row))
    for i in range(depth):
        mods = all_mods[i]
        kind, j = i % N_MIXERS, i // N_MIXERS
        if kind == 0:
            d = DA_HEAD_DIM
            qd = da_h * 2 * d
            tn = _tile(qd, 1024)
            if "da" not in tabs:
                tabs["da"] = _rope_tables(S, B, n_ctx, d, d ** -0.5 * LOG2E)
            qkv = _mm_qkv(h, da_w_qkv, j, tabs["da"], tm_mm, tn, qd // tn, 2 * qd // tn, 1.0, d // 4)
            lam_init = 0.8 - 0.6 * math.exp(-0.3 * i)
            lp = da_lambda[j].astype(F32)
            lam = (jnp.exp(jnp.sum(lp[0] * lp[1])) - jnp.exp(jnp.sum(lp[2] * lp[3])) + lam_init).reshape(1)
            o = _da_attention(qkv, lam, da_subln_g[j], 1.0 - lam_init, B, S, C, da_h, 512)
            w_o = da_w_o
        elif kind == 1:
            d = NA_HEAD_DIM
            tn = _tile(na_h * d, 1024)
            qkv = _mm_qkv(h, na_w_qkv, j, None, tm_mm, tn, na_h * d // tn, 0, d ** -0.5 * LOG2E, 0)
            o = _na_attention(qkv, _na_bias(na_rpb[j], S // GRID_W), B, S, C, na_h)
            w_o = na_w_o
        else:
            d = SW_HEAD_DIM
            nq, nkv = sw_h * d, SW_KV_HEADS * d
            tn = _tile(math.gcd(nq, nkv), 512)
            if "sw" not in tabs:
                tabs["sw"] = _rope_tables(S, B, n_ctx, d, d ** -0.5 * LOG2E)
            qkv = _mm_qkv(h, sw_w_qkv, j, tabs["sw"], tm_mm, tn, nq // tn, (nq + nkv) // tn, 1.0, d // 4)
            o = _sw_attention(qkv, sw_sink[j].astype(F32), B, S, C, sw_h, 256)
            w_o = sw_w_o
        X = _mm_res(o, w_o, j, X, mods, 2, tm_mm, midx(tm_mm))
        hp, eidx, gw = _norm_router(X, norm_g[i, 1], mods, 3, 4, rw_pad, rb, tm_row, midx(tm_row))
        if i + 1 < depth:
            X, h = _moe(X, hp, eidx, gw, moe_w_gate, moe_w_up, moe_w_down, i, mods, 5, midx,
                        next_norm=(norm_g[i + 1, 0], all_mods[i + 1], 0, 1))
        else:
            X = _moe(X, hp, eidx, gw, moe_w_gate, moe_w_up, moe_w_down, i, mods, 5, midx)

    tm_fin = _tile(S, 256, 8)
    return _final_norm(X, final_norm_g, n_lat, tm_fin).reshape(B, S, D)
```

```python
import functools
import math

import numpy as np
import jax
import jax.numpy as jnp
from jax import lax
from jax.experimental import pallas as pl
from jax.experimental.pallas import tpu as pltpu

F32 = jnp.float32
BF16 = jnp.bfloat16
U32 = jnp.uint32
I32 = jnp.int32

GRID_W = 64
N_MIXERS = 3
N_MOD = 6
RMS_EPS = 1e-6
ROPE_THETA = 10000.0
DA_HEAD_DIM = 128
NA_HEAD_DIM = 128
NA_KH = 8
NA_KW = 16
SW_HEAD_DIM = 64
SW_KV_HEADS = 8
SW_WINDOW = 128
N_EXPERTS = 16
N_GROUPS = 4
EXPERTS_PER_GROUP = N_EXPERTS // N_GROUPS

LANES = 128
NEG_BIG = -1e30
NA_ROWS_PER_TILE = 4
NA_HEADS_PER_STEP = 4
DA_STATE_ROWS = 128
LOG2E = math.log2(math.e)
MOE_ROW_TILE = 256
VMEM_MB = 56


def _cp(sem, vmem_mb=VMEM_MB, **kw):
    return pltpu.CompilerParams(dimension_semantics=sem, vmem_limit_bytes=vmem_mb << 20, **kw)


def _tile(n, pref, quantum=LANES):
    if n <= pref:
        return n
    t = (pref // quantum) * quantum
    while t > quantum and n % t:
        t -= quantum
    assert n % t == 0, (n, pref)
    return t


def _dot_t(a, b):
    return lax.dot_general(a, b, (((1,), (1,)), ((), ())), preferred_element_type=F32)


def _dot(a, b):
    return jnp.dot(a, b, preferred_element_type=F32)


def _mod_index_fn(n_lat_tiles, tiles_per_batch, n_batch):
    def f(i):
        return jnp.where(i < n_lat_tiles, i // tiles_per_batch, n_batch)
    return f


def _norm_rows(x, g):
    ms = jnp.mean(x * x, axis=-1, keepdims=True)
    return x * lax.rsqrt(ms + RMS_EPS) * g


def _norm_mod_kernel(x_ref, g_ref, mod_ref, h_ref, *, shift, scale):
    y = _norm_rows(x_ref[...], g_ref[...])
    h = y * (1.0 + mod_ref[0, scale:scale + 1, :]) + mod_ref[0, shift:shift + 1, :]
    h_ref[...] = h.astype(BF16)


def _top2_sum(a, b, c, d):
    hi1, lo1 = jnp.maximum(a, b), jnp.minimum(a, b)
    hi2, lo2 = jnp.maximum(c, d), jnp.minimum(c, d)
    return jnp.maximum(hi1, hi2) + jnp.maximum(jnp.minimum(hi1, hi2), jnp.maximum(lo1, lo2))


def _argmax_first(vals):
    best, idx = vals[0], jnp.zeros(vals[0].shape, I32)
    for k in range(1, len(vals)):
        upd = vals[k] > best
        idx = jnp.where(upd, k, idx)
        best = jnp.where(upd, vals[k], best)
    return idx, best


def _store_token_major(ref, x, first_slab=0):
    n = x.shape[0]
    k = ref.shape[0] // n
    for j in range(x.shape[1] // LANES):
        ref[pl.ds(first_slab + j, n, stride=k), :] = x[:, j * LANES:(j + 1) * LANES]


def _load_token_major(ref, n):
    k = ref.shape[0] // n
    return [ref[pl.ds(j, n, stride=k), :] for j in range(k)]


def _norm_router_kernel(x_ref, g_ref, mod_ref, rw_ref, rb_ref, hp_ref, eidx_ref, gw_ref, *, shift, scale):
    y = _norm_rows(x_ref[...], g_ref[...])
    h = y * (1.0 + mod_ref[0, scale:scale + 1, :]) + mod_ref[0, shift:shift + 1, :]
    hb = h.astype(BF16)
    half = h.shape[1] // 2
    _store_token_major(hp_ref, _pack_bf16_pair(h[:, :half], h[:, half:]))

    logits = _dot(hb, rw_ref[...])
    lt = logits.T[:N_EXPERTS]
    aff = 1.0 / (1.0 + jnp.exp(-lt))
    sel = aff + rb_ref[...]
    sel_rows = [sel[e:e + 1] for e in range(N_EXPERTS)]
    aff_rows = [aff[e:e + 1] for e in range(N_EXPERTS)]
    epg = EXPERTS_PER_GROUP
    gscore = [_top2_sum(*sel_rows[g * epg:(g + 1) * epg]) for g in range(N_GROUPS)]
    bg, _ = _argmax_first(gscore)

    def pick(rows, k):
        out = rows[k]
        for g in range(1, N_GROUPS):
            out = jnp.where(bg == g, rows[g * epg + k], out)
        return out

    s_in = [pick(sel_rows, k) for k in range(epg)]
    a_in = [pick(aff_rows, k) for k in range(epg)]
    i0, _ = _argmax_first(s_in)
    i1, _ = _argmax_first([jnp.where(i0 == k, -jnp.inf, s_in[k]) for k in range(epg)])

    def take(rows, idx):
        out = rows[0]
        for k in range(1, epg):
            out = jnp.where(idx == k, rows[k], out)
        return out

    w0, w1 = take(a_in, i0), take(a_in, i1)
    tot = w0 + w1
    zf = jnp.zeros((6,) + w0.shape[1:], F32)
    gw_ref[...] = jnp.concatenate([w0 / tot, w1 / tot, zf], axis=0)
    eidx_ref[...] = jnp.concatenate([bg * epg + i0, bg * epg + i1, zf.astype(I32)], axis=0)


def _norm_mod(X, g, mods, shift, scale, tm, midx):
    R, D = X.shape
    return pl.pallas_call(
        functools.partial(_norm_mod_kernel, shift=shift, scale=scale),
        grid=(R // tm,),
        in_specs=[pl.BlockSpec((tm, D), lambda i: (i, 0)),
                  pl.BlockSpec((1, D), lambda i: (0, 0)),
                  pl.BlockSpec((1, N_MOD, D), lambda i: (midx(i), 0, 0))],
        out_specs=pl.BlockSpec((tm, D), lambda i: (i, 0)),
        out_shape=jax.ShapeDtypeStruct((R, D), BF16),
        compiler_params=_cp(("arbitrary",)),
        name="norm_mod",
    )(X, g.reshape(1, D), mods)


def _norm_router(X, g, mods, shift, scale, rw_pad, rb, tm, midx):
    R, D = X.shape
    return pl.pallas_call(
        functools.partial(_norm_router_kernel, shift=shift, scale=scale),
        grid=(R // tm,),
        in_specs=[pl.BlockSpec((tm, D), lambda i: (i, 0)),
                  pl.BlockSpec((1, D), lambda i: (0, 0)),
                  pl.BlockSpec((1, N_MOD, D), lambda i: (midx(i), 0, 0)),
                  pl.BlockSpec((D, LANES), lambda i: (0, 0)),
                  pl.BlockSpec((N_EXPERTS, 1), lambda i: (0, 0))],
        out_specs=[pl.BlockSpec((tm * (D // 2 // LANES), LANES), lambda i: (i, 0)),
                   pl.BlockSpec((8, tm), lambda i: (0, i)),
                   pl.BlockSpec((8, tm), lambda i: (0, i))],
        out_shape=[jax.ShapeDtypeStruct((R * (D // 2 // LANES), LANES), U32),
                   jax.ShapeDtypeStruct((8, R), I32),
                   jax.ShapeDtypeStruct((8, R), F32)],
        compiler_params=_cp(("arbitrary",)),
        name="norm_router",
    )(X, g.reshape(1, D), mods, rw_pad, rb)


def _final_norm_kernel(x_ref, g_ref, o_ref):
    o_ref[...] = _norm_rows(x_ref[...], g_ref[...])


def _final_norm(X, g, n_rows, tm):
    D = X.shape[1]
    return pl.pallas_call(
        _final_norm_kernel,
        grid=(n_rows // tm,),
        in_specs=[pl.BlockSpec((tm, D), lambda i: (i, 0)),
                  pl.BlockSpec((1, D), lambda i: (0, 0))],
        out_specs=pl.BlockSpec((tm, D), lambda i: (i, 0)),
        out_shape=jax.ShapeDtypeStruct((n_rows, D), F32),
        compiler_params=_cp(("arbitrary",)),
        name="final_norm",
    )(X, g.reshape(1, D))


MXU_COLS = 256


def _mm_cast(w_ref, wbf_ref):
    @pl.when(pl.program_id(1) == 0)
    def _():
        wbf_ref[...] = w_ref[...].astype(BF16)


def _mm_sub_blocks(a_ref, wbf_ref):
    tn = wbf_ref.shape[1]
    width = min(tn, MXU_COLS)
    for c0 in range(0, tn, width):
        yield c0, _dot(a_ref[...], wbf_ref[:, c0:c0 + width])


def _mm_bias_kernel(a_ref, w_ref, b_ref, o_ref, wbf_ref):
    _mm_cast(w_ref, wbf_ref)
    for c0, acc in _mm_sub_blocks(a_ref, wbf_ref):
        cols = slice(c0, c0 + acc.shape[1])
        o_ref[:, cols] = (acc + b_ref[:, cols]).astype(o_ref.dtype)


def _mm_res_kernel(a_ref, w_ref, r_ref, mod_ref, o_ref, wbf_ref, *, midx):
    _mm_cast(w_ref, wbf_ref)
    for c0, acc in _mm_sub_blocks(a_ref, wbf_ref):
        cols = slice(c0, c0 + acc.shape[1])
        o_ref[:, cols] = r_ref[:, cols] + mod_ref[0, midx:midx + 1, cols] * acc


def _mm_qkv_kernel(a_ref, w_ref, tab_ref, o_ref, wbf_ref, *, n_rope_tiles, n_q_tiles, q_scale, quarter):
    _mm_cast(w_ref, wbf_ref)
    j = pl.program_id(0)

    def store(fn):
        for c0, acc in _mm_sub_blocks(a_ref, wbf_ref):
            for b in range(acc.shape[1] // LANES):
                o_ref[c0 // LANES + b] = fn(acc[:, b * LANES:(b + 1) * LANES]).astype(BF16)

    if n_rope_tiles:
        @pl.when(j < n_rope_tiles)
        def _():
            c, s1, s2 = tab_ref[0, 0], tab_ref[0, 1], tab_ref[0, 2]
            store(lambda x: x * c + pltpu.roll(x, LANES - quarter, 1) * s1 + pltpu.roll(x, quarter, 1) * s2)

        @pl.when(j >= n_rope_tiles)
        def _():
            store(lambda x: x)
    else:
        @pl.when(j < n_q_tiles)
        def _():
            store(lambda x: x * q_scale)

        @pl.when(j >= n_q_tiles)
        def _():
            store(lambda x: x)


def _mm_bias(a, w3, layer, bias, out_dtype, tn_pref=2048):
    M, K = a.shape
    N = w3.shape[2]
    tn = _tile(N, tn_pref)
    return pl.pallas_call(
        _mm_bias_kernel,
        grid=(N // tn, 1),
        in_specs=[pl.BlockSpec((M, K), lambda j, i: (0, 0)),
                  pl.BlockSpec((None, K, tn), lambda j, i: (layer, 0, j), pipeline_mode=pl.Buffered(1)),
                  pl.BlockSpec((1, tn), lambda j, i: (0, j))],
        out_specs=pl.BlockSpec((M, tn), lambda j, i: (0, j)),
        out_shape=jax.ShapeDtypeStruct((M, N), out_dtype),
        scratch_shapes=[pltpu.VMEM((K, tn), BF16)],
        compiler_params=_cp(("arbitrary", "arbitrary")),
        name="mm_bias",
    )(a, w3, bias)


def _mm_res(a, w3, layer, resid, mods, mod_k, tm, midx, tn_pref=1024):
    M, K = a.shape
    N = w3.shape[2]
    tn = _tile(N, tn_pref)
    return pl.pallas_call(
        functools.partial(_mm_res_kernel, midx=mod_k),
        grid=(N // tn, M // tm),
        in_specs=[pl.BlockSpec((tm, K), lambda j, i: (i, 0)),
                  pl.BlockSpec((None, K, tn), lambda j, i: (layer, 0, j), pipeline_mode=pl.Buffered(1)),
                  pl.BlockSpec((tm, tn), lambda j, i: (i, j)),
                  pl.BlockSpec((1, N_MOD, tn), lambda j, i: (midx(i), 0, j))],
        out_specs=pl.BlockSpec((tm, tn), lambda j, i: (i, j)),
        out_shape=jax.ShapeDtypeStruct((M, N), F32),
        scratch_shapes=[pltpu.VMEM((K, tn), BF16)],
        input_output_aliases={2: 0},
        compiler_params=_cp(("arbitrary", "arbitrary")),
        name="mm_res",
    )(a, w3, resid, mods)


def _mm_qkv(a, w3, layer, tabs, tm, tn, n_q_tiles, n_rope_tiles, q_scale, quarter):
    M, K = a.shape
    N = w3.shape[2]
    nblk = tn // LANES
    if tabs is None:
        tabs = jnp.zeros((1, 3, 8, LANES), F32)
        tab_spec = pl.BlockSpec((1, 3, 8, LANES), lambda j, i: (0, 0, 0, 0))
    else:
        tab_spec = pl.BlockSpec((1, 3, tm, LANES), lambda j, i: (jnp.minimum(j // n_q_tiles, 1), 0, i, 0))
    return pl.pallas_call(
        functools.partial(_mm_qkv_kernel, n_rope_tiles=n_rope_tiles, n_q_tiles=n_q_tiles,
                          q_scale=q_scale, quarter=quarter),
        grid=(N // tn, M // tm),
        in_specs=[pl.BlockSpec((tm, K), lambda j, i: (i, 0)),
                  pl.BlockSpec((None, K, tn), lambda j, i: (layer, 0, j), pipeline_mode=pl.Buffered(1)),
                  tab_spec],
        out_specs=pl.BlockSpec((nblk, tm, LANES), lambda j, i: (j, i, 0)),
        out_shape=jax.ShapeDtypeStruct((N // LANES, M, LANES), BF16),
        scratch_shapes=[pltpu.VMEM((K, tn), BF16)],
        compiler_params=_cp(("arbitrary", "arbitrary")),
        name="mm_qkv",
    )(a, w3, tabs)


def _rope_tables(n_lat, n_batch, n_ctx_rows, head_dim, q_scale):
    half, quarter = head_dim // 2, head_dim // 4
    pos = np.arange(n_lat)
    row, col = (pos // GRID_W).astype(np.float32), (pos % GRID_W).astype(np.float32)
    lane = np.arange(LANES) % head_dim
    inv = jnp.asarray(ROPE_THETA, F32) ** (-jnp.arange(quarter, dtype=F32) / quarter)
    p = jnp.where(jnp.asarray(lane // half == 0)[None, :], jnp.asarray(row)[:, None], jnp.asarray(col)[:, None])
    ang = p * inv[np.asarray((lane % half) % quarter)][None, :]
    first = jnp.asarray((lane % half) < quarter)[None, :]
    cos, sin = jnp.cos(ang), jnp.sin(ang)
    t = jnp.stack([cos, jnp.where(first, -sin, 0.0), jnp.where(first, 0.0, sin)])
    t = jnp.tile(t, (1, n_batch, 1))
    ident = jnp.stack([jnp.ones((n_ctx_rows, LANES), F32), jnp.zeros((n_ctx_rows, LANES), F32),
                       jnp.zeros((n_ctx_rows, LANES), F32)])
    t = jnp.concatenate([t, ident], axis=1)
    return jnp.stack([t * q_scale, t])


def _lane_fold(x, op, acc):
    for j in range(x.shape[1] // LANES):
        acc = op(acc, x[:, j * LANES:(j + 1) * LANES])
    return acc


def _da_kernel(lam_ref, q_ref, kc_ref, vc_ref, *rest, has_lat, out_scale, kchunk):
    if has_lat:
        kl_ref, vl_ref, g_ref, o_ref, s0_scr, s1_scr = rest
    else:
        g_ref, _, o_ref, s0_scr, s1_scr = rest
    s_scr = (s0_scr, s1_scr)
    lam = lam_ref[0]
    tq = q_ref.shape[1]
    C = kc_ref.shape[1]
    chunks = [(kc_ref, vc_ref, 0, C, 0)]
    if has_lat:
        chunks += [(kl_ref, vl_ref, j * kchunk, kchunk, C + j * kchunk) for j in range(kl_ref.shape[1] // kchunk)]

    rsub = math.gcd(tq, DA_STATE_ROWS)
    row_blocks = [slice(r, r + rsub) for r in range(0, tq, rsub)]

    def score_chunk(m, chunk, mrun):
        k_ref, _, off, n, col = chunk
        s_scr[m][:, col:col + n] = _dot_t(q_ref[m], k_ref[m, off:off + n, :])
        return [_lane_fold(s_scr[m][rows, col:col + n], jnp.maximum, mr) for rows, mr in zip(row_blocks, mrun)]

    def prob_chunk(m, chunk, mx, state):
        _, v_ref, off, n, col = chunk
        lrun, o = state
        p, lnew = [], []
        for rows, mxr, lr in zip(row_blocks, mx, lrun):
            pr = jnp.exp2(s_scr[m][rows, col:col + n] - mxr)
            lnew.append(_lane_fold(pr, jnp.add, lr))
            p.append(pr.astype(BF16))
        v = jnp.concatenate([v_ref[0, off:off + n, :], v_ref[1, off:off + n, :]], axis=1)
        return lnew, o + _dot(jnp.concatenate(p, axis=0), v)

    def row_max(mrun):
        return [jnp.max(mr, axis=-1, keepdims=True) for mr in mrun]

    def row_sum(lrun):
        return jnp.concatenate([jnp.sum(lr, axis=-1, keepdims=True) for lr in lrun], axis=0)

    m_init = [jnp.full((rsub, LANES), -jnp.inf, F32) for _ in row_blocks]
    st_init = ([jnp.zeros((rsub, LANES), F32) for _ in row_blocks], jnp.zeros((tq, 2 * LANES), F32))
    mrun0 = m_init
    for ch in chunks:
        mrun0 = score_chunk(0, ch, mrun0)
    mx0 = row_max(mrun0)
    mrun1, st0 = m_init, st_init
    for ch in chunks:
        mrun1 = score_chunk(1, ch, mrun1)
        st0 = prob_chunk(0, ch, mx0, st0)
    mx1 = row_max(mrun1)
    st1 = st_init
    for ch in chunks:
        st1 = prob_chunk(1, ch, mx1, st1)
    o = st0[1] * (1.0 / row_sum(st0[0])) - st1[1] * (lam / row_sum(st1[0]))
    o_ref[...] = (_norm_rows(o, g_ref[...]) * out_scale).astype(BF16)


def _da_attention(qkv, lam, subln_g, out_scale, n_batch, S, C, H, tq):
    R = qkv.shape[1]
    n_lat = n_batch * S
    d2 = 2 * DA_HEAD_DIM
    g = subln_g.reshape(1, d2)
    smem = pl.BlockSpec(memory_space=pltpu.SMEM)
    tql = _tile(S, tq, 8)
    kchunk = _tile(S, 512)
    out = pl.pallas_call(
        functools.partial(_da_kernel, has_lat=True, out_scale=out_scale, kchunk=kchunk),
        grid=(n_batch, H, S // tql),
        in_specs=[smem,
                  pl.BlockSpec((2, tql, LANES), lambda b, h, i: (h, b * (S // tql) + i, 0)),
                  pl.BlockSpec((2, C, LANES), lambda b, h, i: (H + h, n_lat // C + b, 0)),
                  pl.BlockSpec((2, C, LANES), lambda b, h, i: (2 * H + h, n_lat // C + b, 0)),
                  pl.BlockSpec((2, S, LANES), lambda b, h, i: (H + h, b, 0)),
                  pl.BlockSpec((2, S, LANES), lambda b, h, i: (2 * H + h, b, 0)),
                  pl.BlockSpec((1, d2), lambda b, h, i: (0, 0))],
        out_specs=pl.BlockSpec((tql, d2), lambda b, h, i: (b * (S // tql) + i, h)),
        out_shape=jax.ShapeDtypeStruct((R, H * d2), BF16),
        scratch_shapes=[pltpu.VMEM((tql, C + S), F32), pltpu.VMEM((tql, C + S), F32)],
        compiler_params=_cp(("arbitrary", "arbitrary", "arbitrary")),
        name="da_attn_lat",
    )(lam, qkv, qkv, qkv, qkv, qkv, g)
    tqc = _tile(C, tq, 8)
    return pl.pallas_call(
        functools.partial(_da_kernel, has_lat=False, out_scale=out_scale, kchunk=kchunk),
        grid=(n_batch, H, C // tqc),
        in_specs=[smem,
                  pl.BlockSpec((2, tqc, LANES), lambda b, h, i: (h, (n_lat + b * C) // tqc + i, 0)),
                  pl.BlockSpec((2, C, LANES), lambda b, h, i: (H + h, n_lat // C + b, 0)),
                  pl.BlockSpec((2, C, LANES), lambda b, h, i: (2 * H + h, n_lat // C + b, 0)),
                  pl.BlockSpec((1, d2), lambda b, h, i: (0, 0)),
                  pl.BlockSpec(memory_space=pl.ANY)],
        out_specs=pl.BlockSpec((tqc, d2), lambda b, h, i: ((n_lat + b * C) // tqc + i, h)),
        out_shape=jax.ShapeDtypeStruct((R, H * d2), BF16),
        scratch_shapes=[pltpu.VMEM((tqc, C), F32), pltpu.VMEM((tqc, C), F32)],
        input_output_aliases={5: 0},
        compiler_params=_cp(("arbitrary", "arbitrary", "arbitrary")),
        name="da_attn_ctx",
    )(lam, qkv, qkv, qkv, g, out)


def _na_geometry(rows):
    kh, rq = min(NA_KH, rows), NA_ROWS_PER_TILE
    kr = kh + rq
    assert rows % rq == 0 and rows >= kr and rows // rq >= 3
    return kh, rq, kr


def _na_bias(rpb, rows):
    kh, rq, kr = _na_geometry(rows)
    W, kw = GRID_W, NA_KW
    n_tiles = rows // rq
    H, n_dr = rpb.shape[0], rpb.shape[1]
    assert LANES % W == 0 and (kr * W) % LANES == 0

    def row_geometry(i):
        r = i * rq + np.arange(rq)
        k = int(np.clip(i * rq - kh // 2, 0, rows - kr)) + np.arange(kr)
        rs = np.clip(r - kh // 2, 0, rows - kh)
        valid = (k[None, :] >= rs[:, None]) & (k[None, :] < rs[:, None] + kh)
        return np.where(valid, k[None, :] - r[:, None] + (NA_KH - 1), n_dr)

    pats = [row_geometry(0), row_geometry(1), row_geometry(n_tiles - 1)]
    for i in range(1, n_tiles - 1):
        assert (row_geometry(i) == pats[1]).all()
    idx = jnp.asarray(np.stack(pats).reshape(-1), I32)

    L = 2 * W - 1
    u = jnp.pad(rpb.astype(F32), ((0, 0), (0, 0), (0, L - rpb.shape[2])))
    t = jnp.tile(u, (1, 1, W))[:, :, :W * (L - 1)].reshape(H, rpb.shape[1], W, L - 1)[..., kw - 1:kw - 1 + W]
    c = np.arange(W)
    cs = np.clip(c - kw // 2, 0, W - kw)
    vcol = (c[None, :] >= cs[:, None]) & (c[None, :] < cs[:, None] + kw)
    t = jnp.where(vcol[None, None], t * LOG2E, NEG_BIG)
    t = jnp.concatenate([t, jnp.full((H, 1, W, W), NEG_BIG, F32)], axis=1)
    per = LANES // W
    tab = jnp.stack([jnp.pad(t, ((0, 0), (0, 0), (0, 0), (s * W, LANES - (s + 1) * W))) for s in range(per)])
    return tab, idx


def _na_kernel(q_ref, kc_ref, vc_ref, *rest, has_loc, rows):
    if has_loc:
        k_ref, v_ref, idx_ref, tab_ref, o_ref = rest
        kh, rq, kr = _na_geometry(rows)
        i = pl.program_id(2)
        n_tiles = rows // rq
        start = jnp.clip(i * rq - kh // 2, 0, rows - kr) * GRID_W
        start = pl.multiple_of(start, GRID_W)
        pat = jnp.where(i == 0, 0, jnp.where(i == n_tiles - 1, 2, 1))
        per = tab_ref.shape[0]

        def bias_of(hb):
            rows_out = []
            for qr in range(rq):
                blocks = []
                for j in range(kr // per):
                    piece = None
                    for s in range(per):
                        dr = idx_ref[(pat * rq + qr) * kr + j * per + s]
                        term = tab_ref[s, hb, pl.ds(dr, 1)][0]
                        piece = term if piece is None else piece + term
                    blocks.append(piece)
                rows_out.append(jnp.concatenate(blocks, axis=1))
            return jnp.concatenate(rows_out, axis=0)
    else:
        _, o_ref = rest
    for hb in range(q_ref.shape[0]):
        q = q_ref[hb]
        sc = _dot_t(q, kc_ref[hb])
        mx = jnp.max(sc, axis=-1, keepdims=True)
        if has_loc:
            sl = _dot_t(q, k_ref[hb, pl.ds(start, kr * GRID_W), :]) + bias_of(hb)
            mx = jnp.maximum(mx, jnp.max(sl, axis=-1, keepdims=True))
        pc = jnp.exp2(sc - mx)
        l = jnp.sum(pc, axis=-1, keepdims=True)
        o = _dot(pc.astype(BF16), vc_ref[hb])
        if has_loc:
            pl_ = jnp.exp2(sl - mx)
            l = l + jnp.sum(pl_, axis=-1, keepdims=True)
            o = o + _dot(pl_.astype(BF16), v_ref[hb, pl.ds(start, kr * GRID_W), :])
        o_ref[:, hb * LANES:(hb + 1) * LANES] = (o * (1.0 / l)).astype(BF16)


def _na_attention(qkv, bias, n_batch, S, C, H):
    R = qkv.shape[1]
    n_lat = n_batch * S
    rows = S // GRID_W
    kh, rq, kr = _na_geometry(rows)
    tq, n_tiles = rq * GRID_W, rows // rq
    d = NA_HEAD_DIM
    tab, idx = bias
    hb = math.gcd(H, NA_HEADS_PER_STEP)
    nhb = H // hb
    out = pl.pallas_call(
        functools.partial(_na_kernel, has_loc=True, rows=rows),
        grid=(n_batch, nhb, n_tiles),
        in_specs=[pl.BlockSpec((hb, tq, LANES), lambda b, h, i: (h, b * n_tiles + i, 0)),
                  pl.BlockSpec((hb, C, LANES), lambda b, h, i: (nhb + h, n_lat // C + b, 0)),
                  pl.BlockSpec((hb, C, LANES), lambda b, h, i: (2 * nhb + h, n_lat // C + b, 0)),
                  pl.BlockSpec((hb, S, LANES), lambda b, h, i: (nhb + h, b, 0)),
                  pl.BlockSpec((hb, S, LANES), lambda b, h, i: (2 * nhb + h, b, 0)),
                  pl.BlockSpec(memory_space=pltpu.SMEM),
                  pl.BlockSpec((tab.shape[0], hb) + tab.shape[2:], lambda b, h, i: (0, h, 0, 0, 0))],
        out_specs=pl.BlockSpec((tq, hb * d), lambda b, h, i: (b * n_tiles + i, h)),
        out_shape=jax.ShapeDtypeStruct((R, H * d), BF16),
        compiler_params=_cp(("arbitrary", "arbitrary", "arbitrary")),
        name="na_attn_lat",
    )(qkv, qkv, qkv, qkv, qkv, idx, tab)
    return pl.pallas_call(
        functools.partial(_na_kernel, has_loc=False, rows=rows),
        grid=(n_batch, nhb, 1),
        in_specs=[pl.BlockSpec((hb, C, LANES), lambda b, h, i: (h, n_lat // C + b, 0)),
                  pl.BlockSpec((hb, C, LANES), lambda b, h, i: (nhb + h, n_lat // C + b, 0)),
                  pl.BlockSpec((hb, C, LANES), lambda b, h, i: (2 * nhb + h, n_lat // C + b, 0)),
                  pl.BlockSpec(memory_space=pl.ANY)],
        out_specs=pl.BlockSpec((C, hb * d), lambda b, h, i: (n_lat // C + b, h)),
        out_shape=jax.ShapeDtypeStruct((R, H * d), BF16),
        input_output_aliases={3: 0},
        compiler_params=_cp(("arbitrary", "arbitrary", "arbitrary")),
        name="na_attn_ctx",
    )(qkv, qkv, qkv, out)


def _sw_kernel(sink_ref, q_ref, kc_ref, vc_ref, *rest, has_loc, tq, band, S, G):
    if has_loc:
        k_ref, v_ref, o_ref = rest
    else:
        _, o_ref = rest
    hd = SW_HEAD_DIM
    per_blk = LANES // hd
    g2n = G // per_blk
    C = kc_ref.shape[1]
    c, i = pl.program_id(1), pl.program_id(2)
    lane = lax.broadcasted_iota(I32, (1, LANES), 1)

    def block_diag(x, par):
        sw = jnp.concatenate([x[:, hd:], x[:, :hd]], axis=1)
        zero = jnp.zeros_like(x)
        top = jnp.where(lane < hd, x if par == 0 else sw, zero)
        bot = jnp.where(lane >= hd, sw if par == 0 else x, zero)
        return jnp.concatenate([top, bot], axis=0)

    if has_loc:
        bstart = pl.multiple_of(jnp.clip(i * tq - SW_WINDOW, 0, S - band), LANES)
        kb = k_ref[0, pl.ds(bstart, band), :]
        vb = v_ref[0, pl.ds(bstart, band), :]
        qpos = i * tq + lax.broadcasted_iota(I32, (tq, 1), 0)
        kpos = bstart + lax.broadcasted_iota(I32, (1, band), 1)
        valid = jnp.abs(kpos - qpos) <= SW_WINDOW

    for par in range(2):
        kc_bd, vc_bd = block_diag(kc_ref[0], par), block_diag(vc_ref[0], par)
        if has_loc:
            kl_bd, vl_bd = block_diag(kb, par), block_diag(vb, par)
        for g2 in range(g2n):
            blk = par * g2n + g2
            q = q_ref[blk]
            sc2 = _dot_t(q, kc_bd)
            if has_loc:
                sl2 = _dot_t(q, kl_bd)
            pcs, pls, rls = [], [], []
            for hh in range(per_blk):
                sink = sink_ref[(2 * c + par) * G + g2 * per_blk + hh] * LOG2E
                sc = sc2[:, hh * C:(hh + 1) * C]
                mx = jnp.maximum(jnp.max(sc, axis=-1, keepdims=True), sink)
                if has_loc:
                    sl = jnp.where(valid, sl2[:, hh * band:(hh + 1) * band], NEG_BIG)
                    mx = jnp.maximum(mx, jnp.max(sl, axis=-1, keepdims=True))
                pc = jnp.exp2(sc - mx)
                l = jnp.sum(pc, axis=-1, keepdims=True) + jnp.exp2(sink - mx)
                pcs.append(pc)
                if has_loc:
                    pl_ = jnp.exp2(sl - mx)
                    l = l + jnp.sum(pl_, axis=-1, keepdims=True)
                    pls.append(pl_)
                rls.append(1.0 / l)
            o = _dot(jnp.concatenate(pcs, axis=1).astype(BF16), vc_bd)
            if has_loc:
                o = o + _dot(jnp.concatenate(pls, axis=1).astype(BF16), vl_bd)
            o = o * jnp.where(lane < hd, rls[0], rls[1])
            o_ref[:, blk * LANES:(blk + 1) * LANES] = o.astype(BF16)


def _sw_attention(qkv, sink, n_batch, S, C, H, tq):
    R = qkv.shape[1]
    n_lat = n_batch * S
    hd, KV = SW_HEAD_DIM, SW_KV_HEADS
    G = H // KV
    nqb, nkb = H * hd // LANES, KV * hd // LANES
    qpk = 2 * G * hd // LANES
    assert KV % 2 == 0 and LANES // hd == 2 and G % 2 == 0
    tq = _tile(S, tq)
    band = tq + 2 * SW_WINDOW
    assert band <= S and SW_WINDOW % LANES == 0
    smem = pl.BlockSpec(memory_space=pltpu.SMEM)
    out = pl.pallas_call(
        functools.partial(_sw_kernel, has_loc=True, tq=tq, band=band, S=S, G=G),
        grid=(n_batch, KV // 2, S // tq),
        in_specs=[smem,
                  pl.BlockSpec((qpk, tq, LANES), lambda b, c, i: (c, b * (S // tq) + i, 0)),
                  pl.BlockSpec((1, C, LANES), lambda b, c, i: (nqb + c, n_lat // C + b, 0)),
                  pl.BlockSpec((1, C, LANES), lambda b, c, i: (nqb + nkb + c, n_lat // C + b, 0)),
                  pl.BlockSpec((1, S, LANES), lambda b, c, i: (nqb + c, b, 0)),
                  pl.BlockSpec((1, S, LANES), lambda b, c, i: (nqb + nkb + c, b, 0))],
        out_specs=pl.BlockSpec((tq, qpk * LANES), lambda b, c, i: (b * (S // tq) + i, c)),
        out_shape=jax.ShapeDtypeStruct((R, H * hd), BF16),
        compiler_params=_cp(("arbitrary", "arbitrary", "arbitrary")),
        name="sw_attn_lat",
    )(sink, qkv, qkv, qkv, qkv, qkv)
    return pl.pallas_call(
        functools.partial(_sw_kernel, has_loc=False, tq=C, band=0, S=S, G=G),
        grid=(n_batch, KV // 2, 1),
        in_specs=[smem,
                  pl.BlockSpec((qpk, C, LANES), lambda b, c, i: (c, n_lat // C + b, 0)),
                  pl.BlockSpec((1, C, LANES), lambda b, c, i: (nqb + c, n_lat // C + b, 0)),
                  pl.BlockSpec((1, C, LANES), lambda b, c, i: (nqb + nkb + c, n_lat // C + b, 0)),
                  pl.BlockSpec(memory_space=pl.ANY)],
        out_specs=pl.BlockSpec((C, qpk * LANES), lambda b, c, i: (n_lat // C + b, c)),
        out_shape=jax.ShapeDtypeStruct((R, H * hd), BF16),
        input_output_aliases={4: 0},
        compiler_params=_cp(("arbitrary", "arbitrary", "arbitrary")),
        name="sw_attn_ctx",
    )(sink, qkv, qkv, qkv, out)


def _moe_plan(eidx, tme):
    e_flat = eidx[:2].reshape(-1)
    onehot = (e_flat[:, None] == jnp.arange(N_EXPERTS, dtype=I32)[None, :]).astype(I32)
    csum = jnp.cumsum(onehot, axis=0)
    rank = jnp.sum(csum * onehot, axis=1) - 1
    cnt = csum[-1]
    pcnt = (cnt + tme - 1) // tme * tme
    ends = jnp.cumsum(pcnt)
    offs = ends - pcnt
    dest = (offs[e_flat] + rank).astype(I32)
    pad_row = jnp.where(cnt % tme != 0, ends - tme, -1).astype(I32)
    return dest, (pcnt // tme).astype(I32), (offs // tme).astype(I32), pad_row


def _work_list(tiles_e, tile_base, n_chunks, n_items_max):
    items_e = tiles_e * n_chunks
    ends = jnp.cumsum(items_e)
    nw = ends[-1]
    w = jnp.minimum(jnp.arange(n_items_max, dtype=I32), nw - 1)
    e = jnp.sum((w[:, None] >= ends[None, :]).astype(I32), axis=1)
    local = w - (ends[e] - items_e[e])
    te = jnp.maximum(tiles_e[e], 1)
    chunk, r = local // te, local % te
    return (e.astype(I32), chunk.astype(I32), (tile_base[e] + r).astype(I32),
            (r == 0).astype(I32), nw.astype(I32).reshape(1))


def _token_copy(src, src_tok, dst, dst_tok, k, sem):
    def first_row(tok):
        return tok * k if isinstance(tok, int) else pl.multiple_of(tok * k, k)

    return pltpu.make_async_copy(src.at[pl.ds(first_row(src_tok), k)], dst.at[pl.ds(first_row(dst_tok), k)], sem)


def _scatter_rows_kernel(dest_ref, pad_ref, hp_ref, o_hbm, zbuf, sem, zsem, *, ts, n_rows, tme, k):
    def zero_copy(e):
        row = pl.multiple_of(pad_ref[e] * k, 8)
        return pltpu.make_async_copy(zbuf, o_hbm.at[pl.ds(row, tme * k)], zsem)

    @pl.when(pl.program_id(0) == 0)
    def _():
        zbuf[...] = jnp.zeros_like(zbuf)
        for e in range(N_EXPERTS):
            pl.when(pad_ref[e] >= 0)(lambda e=e: zero_copy(e).start())
        for e in range(N_EXPERTS):
            pl.when(pad_ref[e] >= 0)(lambda e=e: zero_copy(e).wait())

    base = pl.program_id(0) * ts

    def issue(r, carry):
        for slot in range(2):
            _token_copy(hp_ref, r, o_hbm, dest_ref[slot * n_rows + base + r], k, sem).start()
        return carry

    def drain(r, carry):
        for slot in range(2):
            _token_copy(hp_ref, r, o_hbm, 0, k, sem).wait()
        return carry

    lax.fori_loop(0, ts, issue, 0)
    lax.fori_loop(0, ts, drain, 0)


def _scatter_rows(hp, dest, pad_row, n_rows, tme, p_max):
    k = hp.shape[0] // n_rows
    ts = _tile(n_rows, 256, 8)
    return pl.pallas_call(
        functools.partial(_scatter_rows_kernel, ts=ts, n_rows=n_rows, tme=tme, k=k),
        grid_spec=pltpu.PrefetchScalarGridSpec(
            num_scalar_prefetch=2, grid=(n_rows // ts,),
            in_specs=[pl.BlockSpec((ts * k, LANES), lambda i, d, p: (i, 0))],
            out_specs=pl.BlockSpec(memory_space=pl.ANY),
            scratch_shapes=[pltpu.VMEM((tme * k, LANES), hp.dtype), pltpu.SemaphoreType.DMA(()),
                            pltpu.SemaphoreType.DMA(())]),
        out_shape=jax.ShapeDtypeStruct((p_max * k, LANES), hp.dtype),
        compiler_params=_cp(("arbitrary",)),
        name="moe_scatter",
    )(dest, pad_row, hp)


def _silu(g):
    return g / (1.0 + jnp.exp(-g))


def _moe_up_kernel(ie, inext, it, ifirst, nw, hp_ref, wg_hbm, wu_hbm, a_ref, wg32, wu32, wgb, wub, sem, *, layer):
    w = pl.program_id(0)

    def fetch(e):
        return (pltpu.make_async_copy(wg_hbm.at[layer, e], wg32, sem.at[0]),
                pltpu.make_async_copy(wu_hbm.at[layer, e], wu32, sem.at[1]))

    @pl.when(w == 0)
    def _():
        for copy in fetch(ie[0]):
            copy.start()

    @pl.when(w < nw[0])
    def _():
        @pl.when(ifirst[w] == 1)
        def _():
            for copy in fetch(ie[w]):
                copy.wait()
            wgb[...] = wg32[...].astype(BF16)
            wub[...] = wu32[...].astype(BF16)

            @pl.when(inext[w] >= 0)
            def _():
                for copy in fetch(inext[w]):
                    copy.start()

        hp = jnp.concatenate(_load_token_major(hp_ref, a_ref.shape[0]), axis=1)
        half = hp.shape[1]
        lo, hi = (v.astype(BF16) for v in _unpack_bf16_pair(hp))
        g = _dot(lo, wgb[:half]) + _dot(hi, wgb[half:])
        u = _dot(lo, wub[:half]) + _dot(hi, wub[half:])
        a_ref[...] = (_silu(g) * u).astype(BF16)


def _moe_down_kernel(ie, ic, it, ifirst, nw, a_ref, wd_ref, y_ref, wdb):
    w = pl.program_id(0)

    @pl.when(w < nw[0])
    def _():
        @pl.when(ifirst[w] == 1)
        def _():
            wdb[...] = wd_ref[...].astype(BF16)

        half = wdb.shape[1] // 2
        width = min(half, MXU_COLS)
        a = a_ref[...]
        for c0 in range(0, half, width):
            lo = _dot(a, wdb[:, c0:c0 + width])
            hi = _dot(a, wdb[:, half + c0:half + c0 + width])
            _store_token_major(y_ref, _pack_bf16_pair(lo, hi), c0 // LANES)


def _pack_bf16_pair(lo, hi):
    lo_bits = pltpu.bitcast(lo.astype(BF16).astype(F32), U32)
    hi_bits = pltpu.bitcast(hi.astype(BF16).astype(F32), U32)
    return (lo_bits >> 16) | (hi_bits & jnp.uint32(0xFFFF0000))


def _unpack_bf16_pair(p):
    return pltpu.bitcast(p << 16, F32), pltpu.bitcast(p & jnp.uint32(0xFFFF0000), F32)


def _row_copy(src_hbm, src_row, dst, dst_row, sem):
    return pltpu.make_async_copy(src_hbm.at[pl.ds(src_row, 1)], dst.at[pl.ds(dst_row, 1)], sem)


def _combine_kernel(dest_ref, y_hbm, x_ref, mod_ref, gate_ref, *rest, tc, n_rows, mod_k, next_norm):
    if next_norm:
        g_next_ref, mod_next_ref, o_ref, h_ref, buf, sem = rest
    else:
        o_ref, buf, sem = rest
    i, n_steps = pl.program_id(0), pl.num_programs(0)

    k = buf.shape[2] // tc

    def gather(step, start):
        tile_buf, tile_sem = buf.at[step % 2], sem.at[step % 2]

        def body(r, carry):
            for slot in range(2):
                src_tok = dest_ref[slot * n_rows + step * tc + r] if start else 0
                copy = _token_copy(y_hbm, src_tok, tile_buf.at[slot], r, k, tile_sem)
                copy.start() if start else copy.wait()
            return carry

        lax.fori_loop(0, tc, body, 0)

    pl.when(i == 0)(lambda: gather(i, True))
    pl.when(i + 1 < n_steps)(lambda: gather(i + 1, True))
    gather(i, False)
    cur = buf.at[i % 2]
    half = k * LANES
    g0, g1 = gate_ref[:, 0:1], gate_ref[:, 1:2]
    slabs = zip(_load_token_major(cur.at[0], tc), _load_token_major(cur.at[1], tc))
    for j, (p0, p1) in enumerate(slabs):
        y0_lo, y0_hi = _unpack_bf16_pair(p0)
        y1_lo, y1_hi = _unpack_bf16_pair(p1)
        for off, y in ((0, g0 * y0_lo + g1 * y1_lo), (half, g0 * y0_hi + g1 * y1_hi)):
            out_cols = slice(off + j * LANES, off + (j + 1) * LANES)
            o_ref[:, out_cols] = x_ref[:, out_cols] + mod_ref[0, mod_k:mod_k + 1, out_cols] * y
    if next_norm:
        shift, scale = next_norm
        h = _norm_rows(o_ref[...], g_next_ref[...])
        h_ref[...] = (h * (1.0 + mod_next_ref[0, scale:scale + 1, :]) + mod_next_ref[0, shift:shift + 1, :]).astype(BF16)


def _moe(X, hp, eidx, gw, w_gate, w_up, w_down, layer, mods, mod_k, midx, next_norm=None):
    R, D = X.shape
    d_ff = w_gate.shape[3]
    tme = MOE_ROW_TILE
    p_max = -(-(2 * R + N_EXPERTS * (tme - 1)) // tme) * tme
    t_max = p_max // tme
    dest, tiles_e, tile_base, pad_row = _moe_plan(eidx, tme)
    hs = _scatter_rows(hp, dest, pad_row, R, tme, p_max)
    kp = D // 2 // LANES

    items = _work_list(tiles_e, tile_base, 1, t_max)
    e_ids = jnp.arange(N_EXPERTS, dtype=I32)
    routed_at_or_after = lax.cummin(jnp.where(tiles_e > 0, e_ids, N_EXPERTS)[::-1])[::-1]
    next_routed = jnp.concatenate([routed_at_or_after[1:], jnp.full((1,), N_EXPERTS, I32)])
    next_routed = jnp.where(next_routed < N_EXPERTS, next_routed, -1).astype(I32)
    ie, _, it_, ifirst, nw = items
    a = pl.pallas_call(
        functools.partial(_moe_up_kernel, layer=layer),
        grid_spec=pltpu.PrefetchScalarGridSpec(
            num_scalar_prefetch=5, grid=(t_max,),
            in_specs=[pl.BlockSpec((tme * kp, LANES), lambda w, ie, nx, it, fi, nw: (it[w], 0)),
                      pl.BlockSpec(memory_space=pl.ANY), pl.BlockSpec(memory_space=pl.ANY)],
            out_specs=pl.BlockSpec((tme, d_ff), lambda w, ie, nx, it, fi, nw: (it[w], 0)),
            scratch_shapes=[pltpu.VMEM((D, d_ff), F32), pltpu.VMEM((D, d_ff), F32),
                            pltpu.VMEM((D, d_ff), BF16), pltpu.VMEM((D, d_ff), BF16),
                            pltpu.SemaphoreType.DMA((2,))]),
        out_shape=jax.ShapeDtypeStruct((p_max, d_ff), BF16),
        compiler_params=_cp(("arbitrary",)),
        name="moe_up",
    )(ie, next_routed[ie], it_, ifirst, nw, hs, w_gate, w_up)

    y = pl.pallas_call(
        _moe_down_kernel,
        grid_spec=pltpu.PrefetchScalarGridSpec(
            num_scalar_prefetch=5, grid=(t_max,),
            in_specs=[pl.BlockSpec((tme, d_ff), lambda w, ie, ic, it, fi, nw: (it[w], 0)),
                      pl.BlockSpec((None, None, d_ff, D), lambda w, ie, ic, it, fi, nw: (layer, ie[w], 0, 0))],
            out_specs=pl.BlockSpec((tme * kp, LANES), lambda w, ie, ic, it, fi, nw: (it[w], 0)),
            scratch_shapes=[pltpu.VMEM((d_ff, D), BF16)]),
        out_shape=jax.ShapeDtypeStruct((p_max * kp, LANES), U32),
        compiler_params=_cp(("arbitrary",)),
        name="moe_down",
    )(*items, a, w_down)

    tc = _tile(R, 128, 8)
    mod_of_tile = midx(tc)
    row_spec = pl.BlockSpec((tc, D), lambda i, d: (i, 0))
    mod_spec = pl.BlockSpec((1, N_MOD, D), lambda i, d: (mod_of_tile(i), 0, 0))
    in_specs = [pl.BlockSpec(memory_space=pl.ANY), row_spec, mod_spec, pl.BlockSpec((tc, 2), lambda i, d: (i, 0))]
    operands = [dest, y, X, mods, gw[:2].T]
    out_specs, out_shape = row_spec, jax.ShapeDtypeStruct((R, D), F32)
    if next_norm is not None:
        g_next, mods_next, shift, scale = next_norm
        in_specs += [pl.BlockSpec((1, D), lambda i, d: (0, 0)), mod_spec]
        operands += [g_next.reshape(1, D), mods_next]
        out_specs, out_shape = [row_spec, row_spec], [out_shape, jax.ShapeDtypeStruct((R, D), BF16)]
    return pl.pallas_call(
        functools.partial(_combine_kernel, tc=tc, n_rows=R, mod_k=mod_k,
                          next_norm=None if next_norm is None else (shift, scale)),
        grid_spec=pltpu.PrefetchScalarGridSpec(
            num_scalar_prefetch=1, grid=(R // tc,),
            in_specs=in_specs,
            out_specs=out_specs,
            scratch_shapes=[pltpu.VMEM((2, 2, tc * kp, LANES), U32), pltpu.SemaphoreType.DMA((2,))]),
        out_shape=out_shape,
        input_output_aliases={2: 0},
        compiler_params=_cp(("arbitrary",)),
        name="moe_combine",
    )(*operands)


def kernel(x, c, ctx, c_ctx, cond_down, mod_w, mod_b, norm_g, final_norm_g, da_w_qkv, da_w_o, da_lambda, da_subln_g, na_w_qkv, na_w_o, na_rpb, sw_w_qkv, sw_w_o, sw_sink, router_w, router_bias, moe_w_gate, moe_w_up, moe_w_down):
    B, S, D = x.shape
    C = ctx.shape[1]
    depth = mod_w.shape[0]
    n_lat, n_ctx = B * S, B * C
    R = n_lat + n_ctx

    def midx(tm):
        assert S % tm == 0 and n_ctx % tm == 0
        return _mod_index_fn(n_lat // tm, S // tm, B)

    X = jnp.concatenate([x.reshape(n_lat, D), ctx.reshape(n_ctx, D)], axis=0)

    cond_rows = 16
    cin = jnp.concatenate([c, c_ctx[None, :], jnp.zeros((cond_rows - B - 1, D), F32)], axis=0)
    cin = jax.nn.silu(cin).astype(BF16)
    cond = _mm_bias(cin, cond_down[None], 0, jnp.zeros((1, cond_down.shape[1]), F32), BF16)

    rw_pad = jnp.zeros((D, LANES), F32).at[:, :N_EXPERTS].set(router_w).astype(BF16)
    rb = router_bias.astype(F32).reshape(N_EXPERTS, 1)

    tm_row = _tile(math.gcd(S, n_ctx), 256, 8)
    tm_mm = _tile(math.gcd(S, n_ctx), 512, 8)
    da_h = da_w_qkv.shape[2] // (6 * DA_HEAD_DIM)
    na_h = na_w_qkv.shape[2] // (3 * NA_HEAD_DIM)
    sw_h = sw_sink.shape[1]
    tabs = {}

    all_mods = [_mm_bias(cond, mod_w, i, mod_b[i][None, :], F32)[:B + 1].reshape(B + 1, N_MOD, D)
                for i in range(depth)]
    h = _norm_mod(X, norm_g[0, 0], all_mods[0], 0, 1, tm_row, midx(tm_row))
    for i in range(depth):
        mods = all_mods[i]
        kind, j = i % N_MIXERS, i // N_MIXERS
        if kind == 0:
            d = DA_HEAD_DIM
            qd = da_h * 2 * d
            tn = _tile(qd, 1024)
            if "da" not in tabs:
                tabs["da"] = _rope_tables(S, B, n_ctx, d, d ** -0.5 * LOG2E)
            qkv = _mm_qkv(h, da_w_qkv, j, tabs["da"], tm_mm, tn, qd // tn, 2 * qd // tn, 1.0, d // 4)
            lam_init = 0.8 - 0.6 * math.exp(-0.3 * i)
            lp = da_lambda[j].astype(F32)
            lam = (jnp.exp(jnp.sum(lp[0] * lp[1])) - jnp.exp(jnp.sum(lp[2] * lp[3])) + lam_init).reshape(1)
            o = _da_attention(qkv, lam, da_subln_g[j], 1.0 - lam_init, B, S, C, da_h, 512)
            w_o = da_w_o
        elif kind == 1:
            d = NA_HEAD_DIM
            tn = _tile(na_h * d, 1024)
            qkv = _mm_qkv(h, na_w_qkv, j, None, tm_mm, tn, na_h * d // tn, 0, d ** -0.5 * LOG2E, 0)
            o = _na_attention(qkv, _na_bias(na_rpb[j], S // GRID_W), B, S, C, na_h)
            w_o = na_w_o
        else:
            d = SW_HEAD_DIM
            nq, nkv = sw_h * d, SW_KV_HEADS * d
            tn = _tile(math.gcd(nq, nkv), 512)
            if "sw" not in tabs:
                tabs["sw"] = _rope_tables(S, B, n_ctx, d, d ** -0.5 * LOG2E)
            qkv = _mm_qkv(h, sw_w_qkv, j, tabs["sw"], tm_mm, tn, nq // tn, (nq + nkv) // tn, 1.0, d // 4)
            o = _sw_attention(qkv, sw_sink[j].astype(F32), B, S, C, sw_h, 256)
            w_o = sw_w_o
        X = _mm_res(o, w_o, j, X, mods, 2, tm_mm, midx(tm_mm))
        hp, eidx, gw = _norm_router(X, norm_g[i, 1], mods, 3, 4, rw_pad, rb, tm_row, midx(tm_row))
        if i + 1 < depth:
            X, h = _moe(X, hp, eidx, gw, moe_w_gate, moe_w_up, moe_w_down, i, mods, 5, midx,
                        next_norm=(norm_g[i + 1, 0], all_mods[i + 1], 0, 1))
        else:
            X = _moe(X, hp, eidx, gw, moe_w_gate, moe_w_up, moe_w_down, i, mods, 5, midx)

    tm_fin = _tile(S, 256, 8)
    return _final_norm(X, final_norm_g, n_lat, tm_fin).reshape(B, S, D)
```

```python
import functools
import math

import numpy as np
import jax
import jax.numpy as jnp
from jax import lax
from jax.experimental import pallas as pl
from jax.experimental.pallas import tpu as pltpu

F32 = jnp.float32
BF16 = jnp.bfloat16
U32 = jnp.uint32
I32 = jnp.int32

GRID_W = 64
N_MIXERS = 3
N_MOD = 6
RMS_EPS = 1e-6
ROPE_THETA = 10000.0
DA_HEAD_DIM = 128
NA_HEAD_DIM = 128
NA_KH = 8
NA_KW = 16
SW_HEAD_DIM = 64
SW_KV_HEADS = 8
SW_WINDOW = 128
N_EXPERTS = 16
N_GROUPS = 4
EXPERTS_PER_GROUP = N_EXPERTS // N_GROUPS

LANES = 128
NEG_BIG = -1e30
NA_ROWS_PER_TILE = 4
NA_HEADS_PER_STEP = 4
DA_STATE_ROWS = 128
LOG2E = math.log2(math.e)
MOE_ROW_TILE = 256
VMEM_MB = 56


def _cp(sem, vmem_mb=VMEM_MB, **kw):
    return pltpu.CompilerParams(dimension_semantics=sem, vmem_limit_bytes=vmem_mb << 20, **kw)


def _tile(n, pref, quantum=LANES):
    if n <= pref:
        return n
    t = (pref // quantum) * quantum
    while t > quantum and n % t:
        t -= quantum
    assert n % t == 0, (n, pref)
    return t


def _dot_t(a, b):
    return lax.dot_general(a, b, (((1,), (1,)), ((), ())), preferred_element_type=F32)


def _dot(a, b):
    return jnp.dot(a, b, preferred_element_type=F32)


def _mod_index_fn(n_lat_tiles, tiles_per_batch, n_batch):
    def f(i):
        return jnp.where(i < n_lat_tiles, i // tiles_per_batch, n_batch)
    return f


def _norm_rows(x, g):
    ms = jnp.mean(x * x, axis=-1, keepdims=True)
    return x * lax.rsqrt(ms + RMS_EPS) * g


def _norm_mod_kernel(x_ref, g_ref, mod_ref, h_ref, *, shift, scale):
    y = _norm_rows(x_ref[...], g_ref[...])
    h = y * (1.0 + mod_ref[0, scale:scale + 1, :]) + mod_ref[0, shift:shift + 1, :]
    h_ref[...] = h.astype(BF16)


def _top2_sum(a, b, c, d):
    hi1, lo1 = jnp.maximum(a, b), jnp.minimum(a, b)
    hi2, lo2 = jnp.maximum(c, d), jnp.minimum(c, d)
    return jnp.maximum(hi1, hi2) + jnp.maximum(jnp.minimum(hi1, hi2), jnp.maximum(lo1, lo2))


def _argmax_first(vals):
    best, idx = vals[0], jnp.zeros(vals[0].shape, I32)
    for k in range(1, len(vals)):
        upd = vals[k] > best
        idx = jnp.where(upd, k, idx)
        best = jnp.where(upd, vals[k], best)
    return idx, best


def _store_token_major(ref, x):
    n, k = x.shape[0], x.shape[1] // LANES
    for j in range(k):
        ref[pl.ds(j, n, stride=k), :] = x[:, j * LANES:(j + 1) * LANES]


def _load_token_major(ref, n):
    k = ref.shape[0] // n
    return [ref[pl.ds(j, n, stride=k), :] for j in range(k)]


def _norm_router_kernel(x_ref, g_ref, mod_ref, rw_ref, rb_ref, hp_ref, eidx_ref, gw_ref, *, shift, scale):
    y = _norm_rows(x_ref[...], g_ref[...])
    h = y * (1.0 + mod_ref[0, scale:scale + 1, :]) + mod_ref[0, shift:shift + 1, :]
    hb = h.astype(BF16)
    half = h.shape[1] // 2
    _store_token_major(hp_ref, _pack_bf16_pair(h[:, :half], h[:, half:]))

    logits = _dot(hb, rw_ref[...])
    lt = logits.T[:N_EXPERTS]
    aff = 1.0 / (1.0 + jnp.exp(-lt))
    sel = aff + rb_ref[...]
    sel_rows = [sel[e:e + 1] for e in range(N_EXPERTS)]
    aff_rows = [aff[e:e + 1] for e in range(N_EXPERTS)]
    epg = EXPERTS_PER_GROUP
    gscore = [_top2_sum(*sel_rows[g * epg:(g + 1) * epg]) for g in range(N_GROUPS)]
    bg, _ = _argmax_first(gscore)

    def pick(rows, k):
        out = rows[k]
        for g in range(1, N_GROUPS):
            out = jnp.where(bg == g, rows[g * epg + k], out)
        return out

    s_in = [pick(sel_rows, k) for k in range(epg)]
    a_in = [pick(aff_rows, k) for k in range(epg)]
    i0, _ = _argmax_first(s_in)
    i1, _ = _argmax_first([jnp.where(i0 == k, -jnp.inf, s_in[k]) for k in range(epg)])

    def take(rows, idx):
        out = rows[0]
        for k in range(1, epg):
            out = jnp.where(idx == k, rows[k], out)
        return out

    w0, w1 = take(a_in, i0), take(a_in, i1)
    tot = w0 + w1
    zf = jnp.zeros((6,) + w0.shape[1:], F32)
    gw_ref[...] = jnp.concatenate([w0 / tot, w1 / tot, zf], axis=0)
    eidx_ref[...] = jnp.concatenate([bg * epg + i0, bg * epg + i1, zf.astype(I32)], axis=0)


def _norm_mod(X, g, mods, shift, scale, tm, midx):
    R, D = X.shape
    return pl.pallas_call(
        functools.partial(_norm_mod_kernel, shift=shift, scale=scale),
        grid=(R // tm,),
        in_specs=[pl.BlockSpec((tm, D), lambda i: (i, 0)),
                  pl.BlockSpec((1, D), lambda i: (0, 0)),
                  pl.BlockSpec((1, N_MOD, D), lambda i: (midx(i), 0, 0))],
        out_specs=pl.BlockSpec((tm, D), lambda i: (i, 0)),
        out_shape=jax.ShapeDtypeStruct((R, D), BF16),
        compiler_params=_cp(("arbitrary",)),
        name="norm_mod",
    )(X, g.reshape(1, D), mods)


def _norm_router(X, g, mods, shift, scale, rw_pad, rb, tm, midx):
    R, D = X.shape
    return pl.pallas_call(
        functools.partial(_norm_router_kernel, shift=shift, scale=scale),
        grid=(R // tm,),
        in_specs=[pl.BlockSpec((tm, D), lambda i: (i, 0)),
                  pl.BlockSpec((1, D), lambda i: (0, 0)),
                  pl.BlockSpec((1, N_MOD, D), lambda i: (midx(i), 0, 0)),
                  pl.BlockSpec((D, LANES), lambda i: (0, 0)),
                  pl.BlockSpec((N_EXPERTS, 1), lambda i: (0, 0))],
        out_specs=[pl.BlockSpec((tm * (D // 2 // LANES), LANES), lambda i: (i, 0)),
                   pl.BlockSpec((8, tm), lambda i: (0, i)),
                   pl.BlockSpec((8, tm), lambda i: (0, i))],
        out_shape=[jax.ShapeDtypeStruct((R * (D // 2 // LANES), LANES), U32),
                   jax.ShapeDtypeStruct((8, R), I32),
                   jax.ShapeDtypeStruct((8, R), F32)],
        compiler_params=_cp(("arbitrary",)),
        name="norm_router",
    )(X, g.reshape(1, D), mods, rw_pad, rb)


def _final_norm_kernel(x_ref, g_ref, o_ref):
    o_ref[...] = _norm_rows(x_ref[...], g_ref[...])


def _final_norm(X, g, n_rows, tm):
    D = X.shape[1]
    return pl.pallas_call(
        _final_norm_kernel,
        grid=(n_rows // tm,),
        in_specs=[pl.BlockSpec((tm, D), lambda i: (i, 0)),
                  pl.BlockSpec((1, D), lambda i: (0, 0))],
        out_specs=pl.BlockSpec((tm, D), lambda i: (i, 0)),
        out_shape=jax.ShapeDtypeStruct((n_rows, D), F32),
        compiler_params=_cp(("arbitrary",)),
        name="final_norm",
    )(X, g.reshape(1, D))


MXU_COLS = 256


def _mm_cast(w_ref, wbf_ref):
    @pl.when(pl.program_id(1) == 0)
    def _():
        wbf_ref[...] = w_ref[...].astype(BF16)


def _mm_sub_blocks(a_ref, wbf_ref):
    tn = wbf_ref.shape[1]
    width = min(tn, MXU_COLS)
    for c0 in range(0, tn, width):
        yield c0, _dot(a_ref[...], wbf_ref[:, c0:c0 + width])


def _mm_bias_kernel(a_ref, w_ref, b_ref, o_ref, wbf_ref):
    _mm_cast(w_ref, wbf_ref)
    for c0, acc in _mm_sub_blocks(a_ref, wbf_ref):
        cols = slice(c0, c0 + acc.shape[1])
        o_ref[:, cols] = (acc + b_ref[:, cols]).astype(o_ref.dtype)


def _mm_res_kernel(a_ref, w_ref, r_ref, mod_ref, o_ref, wbf_ref, *, midx):
    _mm_cast(w_ref, wbf_ref)
    for c0, acc in _mm_sub_blocks(a_ref, wbf_ref):
        cols = slice(c0, c0 + acc.shape[1])
        o_ref[:, cols] = r_ref[:, cols] + mod_ref[0, midx:midx + 1, cols] * acc


def _mm_qkv_kernel(a_ref, w_ref, tab_ref, o_ref, wbf_ref, *, n_rope_tiles, n_q_tiles, q_scale, quarter):
    _mm_cast(w_ref, wbf_ref)
    j = pl.program_id(0)

    def store(fn):
        for c0, acc in _mm_sub_blocks(a_ref, wbf_ref):
            for b in range(acc.shape[1] // LANES):
                o_ref[c0 // LANES + b] = fn(acc[:, b * LANES:(b + 1) * LANES]).astype(BF16)

    if n_rope_tiles:
        @pl.when(j < n_rope_tiles)
        def _():
            c, s1, s2 = tab_ref[0, 0], tab_ref[0, 1], tab_ref[0, 2]
            store(lambda x: x * c + pltpu.roll(x, LANES - quarter, 1) * s1 + pltpu.roll(x, quarter, 1) * s2)

        @pl.when(j >= n_rope_tiles)
        def _():
            store(lambda x: x)
    else:
        @pl.when(j < n_q_tiles)
        def _():
            store(lambda x: x * q_scale)

        @pl.when(j >= n_q_tiles)
        def _():
            store(lambda x: x)


def _mm_bias(a, w3, layer, bias, out_dtype, tn_pref=2048):
    M, K = a.shape
    N = w3.shape[2]
    tn = _tile(N, tn_pref)
    return pl.pallas_call(
        _mm_bias_kernel,
        grid=(N // tn, 1),
        in_specs=[pl.BlockSpec((M, K), lambda j, i: (0, 0)),
                  pl.BlockSpec((None, K, tn), lambda j, i: (layer, 0, j), pipeline_mode=pl.Buffered(1)),
                  pl.BlockSpec((1, tn), lambda j, i: (0, j))],
        out_specs=pl.BlockSpec((M, tn), lambda j, i: (0, j)),
        out_shape=jax.ShapeDtypeStruct((M, N), out_dtype),
        scratch_shapes=[pltpu.VMEM((K, tn), BF16)],
        compiler_params=_cp(("arbitrary", "arbitrary")),
        name="mm_bias",
    )(a, w3, bias)


def _mm_res(a, w3, layer, resid, mods, mod_k, tm, midx, tn_pref=1024):
    M, K = a.shape
    N = w3.shape[2]
    tn = _tile(N, tn_pref)
    return pl.pallas_call(
        functools.partial(_mm_res_kernel, midx=mod_k),
        grid=(N // tn, M // tm),
        in_specs=[pl.BlockSpec((tm, K), lambda j, i: (i, 0)),
                  pl.BlockSpec((None, K, tn), lambda j, i: (layer, 0, j), pipeline_mode=pl.Buffered(1)),
                  pl.BlockSpec((tm, tn), lambda j, i: (i, j)),
                  pl.BlockSpec((1, N_MOD, tn), lambda j, i: (midx(i), 0, j))],
        out_specs=pl.BlockSpec((tm, tn), lambda j, i: (i, j)),
        out_shape=jax.ShapeDtypeStruct((M, N), F32),
        scratch_shapes=[pltpu.VMEM((K, tn), BF16)],
        input_output_aliases={2: 0},
        compiler_params=_cp(("arbitrary", "arbitrary")),
        name="mm_res",
    )(a, w3, resid, mods)


def _mm_qkv(a, w3, layer, tabs, tm, tn, n_q_tiles, n_rope_tiles, q_scale, quarter):
    M, K = a.shape
    N = w3.shape[2]
    nblk = tn // LANES
    if tabs is None:
        tabs = jnp.zeros((1, 3, 8, LANES), F32)
        tab_spec = pl.BlockSpec((1, 3, 8, LANES), lambda j, i: (0, 0, 0, 0))
    else:
        tab_spec = pl.BlockSpec((1, 3, tm, LANES), lambda j, i: (jnp.minimum(j // n_q_tiles, 1), 0, i, 0))
    return pl.pallas_call(
        functools.partial(_mm_qkv_kernel, n_rope_tiles=n_rope_tiles, n_q_tiles=n_q_tiles,
                          q_scale=q_scale, quarter=quarter),
        grid=(N // tn, M // tm),
        in_specs=[pl.BlockSpec((tm, K), lambda j, i: (i, 0)),
                  pl.BlockSpec((None, K, tn), lambda j, i: (layer, 0, j), pipeline_mode=pl.Buffered(1)),
                  tab_spec],
        out_specs=pl.BlockSpec((nblk, tm, LANES), lambda j, i: (j, i, 0)),
        out_shape=jax.ShapeDtypeStruct((N // LANES, M, LANES), BF16),
        scratch_shapes=[pltpu.VMEM((K, tn), BF16)],
        compiler_params=_cp(("arbitrary", "arbitrary")),
        name="mm_qkv",
    )(a, w3, tabs)


def _rope_tables(n_lat, n_batch, n_ctx_rows, head_dim, q_scale):
    half, quarter = head_dim // 2, head_dim // 4
    pos = np.arange(n_lat)
    row, col = (pos // GRID_W).astype(np.float32), (pos % GRID_W).astype(np.float32)
    lane = np.arange(LANES) % head_dim
    inv = jnp.asarray(ROPE_THETA, F32) ** (-jnp.arange(quarter, dtype=F32) / quarter)
    p = jnp.where(jnp.asarray(lane // half == 0)[None, :], jnp.asarray(row)[:, None], jnp.asarray(col)[:, None])
    ang = p * inv[np.asarray((lane % half) % quarter)][None, :]
    first = jnp.asarray((lane % half) < quarter)[None, :]
    cos, sin = jnp.cos(ang), jnp.sin(ang)
    t = jnp.stack([cos, jnp.where(first, -sin, 0.0), jnp.where(first, 0.0, sin)])
    t = jnp.tile(t, (1, n_batch, 1))
    ident = jnp.stack([jnp.ones((n_ctx_rows, LANES), F32), jnp.zeros((n_ctx_rows, LANES), F32),
                       jnp.zeros((n_ctx_rows, LANES), F32)])
    t = jnp.concatenate([t, ident], axis=1)
    return jnp.stack([t * q_scale, t])


def _lane_fold(x, op, acc):
    for j in range(x.shape[1] // LANES):
        acc = op(acc, x[:, j * LANES:(j + 1) * LANES])
    return acc


def _da_kernel(lam_ref, q_ref, kc_ref, vc_ref, *rest, has_lat, out_scale, kchunk):
    if has_lat:
        kl_ref, vl_ref, g_ref, o_ref, s0_scr, s1_scr = rest
    else:
        g_ref, _, o_ref, s0_scr, s1_scr = rest
    s_scr = (s0_scr, s1_scr)
    lam = lam_ref[0]
    tq = q_ref.shape[1]
    C = kc_ref.shape[1]
    chunks = [(kc_ref, vc_ref, 0, C, 0)]
    if has_lat:
        chunks += [(kl_ref, vl_ref, j * kchunk, kchunk, C + j * kchunk) for j in range(kl_ref.shape[1] // kchunk)]

    rsub = math.gcd(tq, DA_STATE_ROWS)
    row_blocks = [slice(r, r + rsub) for r in range(0, tq, rsub)]

    def score_chunk(m, chunk, mrun):
        k_ref, _, off, n, col = chunk
        s = _dot_t(q_ref[m], k_ref[m, off:off + n, :])
        s_scr[m][:, col:col + n] = s
        return [_lane_fold(s[rows], jnp.maximum, mr) for rows, mr in zip(row_blocks, mrun)]

    def prob_chunk(m, chunk, mx, state):
        _, v_ref, off, n, col = chunk
        lrun, o = state
        p, lnew = [], []
        for rows, mxr, lr in zip(row_blocks, mx, lrun):
            pr = jnp.exp2(s_scr[m][rows, col:col + n] - mxr)
            lnew.append(_lane_fold(pr, jnp.add, lr))
            p.append(pr.astype(BF16))
        v = jnp.concatenate([v_ref[0, off:off + n, :], v_ref[1, off:off + n, :]], axis=1)
        return lnew, o + _dot(jnp.concatenate(p, axis=0), v)

    def row_max(mrun):
        return [jnp.max(mr, axis=-1, keepdims=True) for mr in mrun]

    def row_sum(lrun):
        return jnp.concatenate([jnp.sum(lr, axis=-1, keepdims=True) for lr in lrun], axis=0)

    m_init = [jnp.full((rsub, LANES), -jnp.inf, F32) for _ in row_blocks]
    st_init = ([jnp.zeros((rsub, LANES), F32) for _ in row_blocks], jnp.zeros((tq, 2 * LANES), F32))
    mrun0 = m_init
    for ch in chunks:
        mrun0 = score_chunk(0, ch, mrun0)
    mx0 = row_max(mrun0)
    mrun1, st0 = m_init, st_init
    for ch in chunks:
        mrun1 = score_chunk(1, ch, mrun1)
        st0 = prob_chunk(0, ch, mx0, st0)
    mx1 = row_max(mrun1)
    st1 = st_init
    for ch in chunks:
        st1 = prob_chunk(1, ch, mx1, st1)
    o = st0[1] * (1.0 / row_sum(st0[0])) - st1[1] * (lam / row_sum(st1[0]))
    o_ref[...] = (_norm_rows(o, g_ref[...]) * out_scale).astype(BF16)


def _da_attention(qkv, lam, subln_g, out_scale, n_batch, S, C, H, tq):
    R = qkv.shape[1]
    n_lat = n_batch * S
    d2 = 2 * DA_HEAD_DIM
    g = subln_g.reshape(1, d2)
    smem = pl.BlockSpec(memory_space=pltpu.SMEM)
    tql = _tile(S, tq, 8)
    kchunk = _tile(S, 512)
    out = pl.pallas_call(
        functools.partial(_da_kernel, has_lat=True, out_scale=out_scale, kchunk=kchunk),
        grid=(n_batch, H, S // tql),
        in_specs=[smem,
                  pl.BlockSpec((2, tql, LANES), lambda b, h, i: (h, b * (S // tql) + i, 0)),
                  pl.BlockSpec((2, C, LANES), lambda b, h, i: (H + h, n_lat // C + b, 0)),
                  pl.BlockSpec((2, C, LANES), lambda b, h, i: (2 * H + h, n_lat // C + b, 0)),
                  pl.BlockSpec((2, S, LANES), lambda b, h, i: (H + h, b, 0)),
                  pl.BlockSpec((2, S, LANES), lambda b, h, i: (2 * H + h, b, 0)),
                  pl.BlockSpec((1, d2), lambda b, h, i: (0, 0))],
        out_specs=pl.BlockSpec((tql, d2), lambda b, h, i: (b * (S // tql) + i, h)),
        out_shape=jax.ShapeDtypeStruct((R, H * d2), BF16),
        scratch_shapes=[pltpu.VMEM((tql, C + S), F32), pltpu.VMEM((tql, C + S), F32)],
        compiler_params=_cp(("arbitrary", "arbitrary", "arbitrary")),
        name="da_attn_lat",
    )(lam, qkv, qkv, qkv, qkv, qkv, g)
    tqc = _tile(C, tq, 8)
    return pl.pallas_call(
        functools.partial(_da_kernel, has_lat=False, out_scale=out_scale, kchunk=kchunk),
        grid=(n_batch, H, C // tqc),
        in_specs=[smem,
                  pl.BlockSpec((2, tqc, LANES), lambda b, h, i: (h, (n_lat + b * C) // tqc + i, 0)),
                  pl.BlockSpec((2, C, LANES), lambda b, h, i: (H + h, n_lat // C + b, 0)),
                  pl.BlockSpec((2, C, LANES), lambda b, h, i: (2 * H + h, n_lat // C + b, 0)),
                  pl.BlockSpec((1, d2), lambda b, h, i: (0, 0)),
                  pl.BlockSpec(memory_space=pl.ANY)],
        out_specs=pl.BlockSpec((tqc, d2), lambda b, h, i: ((n_lat + b * C) // tqc + i, h)),
        out_shape=jax.ShapeDtypeStruct((R, H * d2), BF16),
        scratch_shapes=[pltpu.VMEM((tqc, C), F32), pltpu.VMEM((tqc, C), F32)],
        input_output_aliases={5: 0},
        compiler_params=_cp(("arbitrary", "arbitrary", "arbitrary")),
        name="da_attn_ctx",
    )(lam, qkv, qkv, qkv, g, out)


def _na_geometry(rows):
    kh, rq = min(NA_KH, rows), NA_ROWS_PER_TILE
    kr = kh + rq
    assert rows % rq == 0 and rows >= kr and rows // rq >= 3
    return kh, rq, kr


def _na_bias(rpb, rows):
    kh, rq, kr = _na_geometry(rows)
    W, kw = GRID_W, NA_KW
    n_tiles = rows // rq
    H, n_dr = rpb.shape[0], rpb.shape[1]
    assert LANES % W == 0 and (kr * W) % LANES == 0

    def row_geometry(i):
        r = i * rq + np.arange(rq)
        k = int(np.clip(i * rq - kh // 2, 0, rows - kr)) + np.arange(kr)
        rs = np.clip(r - kh // 2, 0, rows - kh)
        valid = (k[None, :] >= rs[:, None]) & (k[None, :] < rs[:, None] + kh)
        return np.where(valid, k[None, :] - r[:, None] + (NA_KH - 1), n_dr)

    pats = [row_geometry(0), row_geometry(1), row_geometry(n_tiles - 1)]
    for i in range(1, n_tiles - 1):
        assert (row_geometry(i) == pats[1]).all()
    idx = jnp.asarray(np.stack(pats).reshape(-1), I32)

    L = 2 * W - 1
    u = jnp.pad(rpb.astype(F32), ((0, 0), (0, 0), (0, L - rpb.shape[2])))
    t = jnp.tile(u, (1, 1, W))[:, :, :W * (L - 1)].reshape(H, rpb.shape[1], W, L - 1)[..., kw - 1:kw - 1 + W]
    c = np.arange(W)
    cs = np.clip(c - kw // 2, 0, W - kw)
    vcol = (c[None, :] >= cs[:, None]) & (c[None, :] < cs[:, None] + kw)
    t = jnp.where(vcol[None, None], t * LOG2E, NEG_BIG)
    t = jnp.concatenate([t, jnp.full((H, 1, W, W), NEG_BIG, F32)], axis=1)
    per = LANES // W
    tab = jnp.stack([jnp.pad(t, ((0, 0), (0, 0), (0, 0), (s * W, LANES - (s + 1) * W))) for s in range(per)])
    return tab, idx


def _na_kernel(q_ref, kc_ref, vc_ref, *rest, has_loc, rows):
    if has_loc:
        k_ref, v_ref, idx_ref, tab_ref, o_ref = rest
        kh, rq, kr = _na_geometry(rows)
        i = pl.program_id(2)
        n_tiles = rows // rq
        start = jnp.clip(i * rq - kh // 2, 0, rows - kr) * GRID_W
        start = pl.multiple_of(start, GRID_W)
        pat = jnp.where(i == 0, 0, jnp.where(i == n_tiles - 1, 2, 1))
        per = tab_ref.shape[0]

        def bias_of(hb):
            rows_out = []
            for qr in range(rq):
                blocks = []
                for j in range(kr // per):
                    piece = None
                    for s in range(per):
                        dr = idx_ref[(pat * rq + qr) * kr + j * per + s]
                        term = tab_ref[s, hb, pl.ds(dr, 1)][0]
                        piece = term if piece is None else piece + term
                    blocks.append(piece)
                rows_out.append(jnp.concatenate(blocks, axis=1))
            return jnp.concatenate(rows_out, axis=0)
    else:
        _, o_ref = rest
    for hb in range(q_ref.shape[0]):
        q = q_ref[hb]
        sc = _dot_t(q, kc_ref[hb])
        mx = jnp.max(sc, axis=-1, keepdims=True)
        if has_loc:
            sl = _dot_t(q, k_ref[hb, pl.ds(start, kr * GRID_W), :]) + bias_of(hb)
            mx = jnp.maximum(mx, jnp.max(sl, axis=-1, keepdims=True))
        pc = jnp.exp2(sc - mx)
        l = jnp.sum(pc, axis=-1, keepdims=True)
        o = _dot(pc.astype(BF16), vc_ref[hb])
        if has_loc:
            pl_ = jnp.exp2(sl - mx)
            l = l + jnp.sum(pl_, axis=-1, keepdims=True)
            o = o + _dot(pl_.astype(BF16), v_ref[hb, pl.ds(start, kr * GRID_W), :])
        o_ref[:, hb * LANES:(hb + 1) * LANES] = (o * (1.0 / l)).astype(BF16)


def _na_attention(qkv, bias, n_batch, S, C, H):
    R = qkv.shape[1]
    n_lat = n_batch * S
    rows = S // GRID_W
    kh, rq, kr = _na_geometry(rows)
    tq, n_tiles = rq * GRID_W, rows // rq
    d = NA_HEAD_DIM
    tab, idx = bias
    hb = math.gcd(H, NA_HEADS_PER_STEP)
    nhb = H // hb
    out = pl.pallas_call(
        functools.partial(_na_kernel, has_loc=True, rows=rows),
        grid=(n_batch, nhb, n_tiles),
        in_specs=[pl.BlockSpec((hb, tq, LANES), lambda b, h, i: (h, b * n_tiles + i, 0)),
                  pl.BlockSpec((hb, C, LANES), lambda b, h, i: (nhb + h, n_lat // C + b, 0)),
                  pl.BlockSpec((hb, C, LANES), lambda b, h, i: (2 * nhb + h, n_lat // C + b, 0)),
                  pl.BlockSpec((hb, S, LANES), lambda b, h, i: (nhb + h, b, 0)),
                  pl.BlockSpec((hb, S, LANES), lambda b, h, i: (2 * nhb + h, b, 0)),
                  pl.BlockSpec(memory_space=pltpu.SMEM),
                  pl.BlockSpec((tab.shape[0], hb) + tab.shape[2:], lambda b, h, i: (0, h, 0, 0, 0))],
        out_specs=pl.BlockSpec((tq, hb * d), lambda b, h, i: (b * n_tiles + i, h)),
        out_shape=jax.ShapeDtypeStruct((R, H * d), BF16),
        compiler_params=_cp(("arbitrary", "arbitrary", "arbitrary")),
        name="na_attn_lat",
    )(qkv, qkv, qkv, qkv, qkv, idx, tab)
    return pl.pallas_call(
        functools.partial(_na_kernel, has_loc=False, rows=rows),
        grid=(n_batch, nhb, 1),
        in_specs=[pl.BlockSpec((hb, C, LANES), lambda b, h, i: (h, n_lat // C + b, 0)),
                  pl.BlockSpec((hb, C, LANES), lambda b, h, i: (nhb + h, n_lat // C + b, 0)),
                  pl.BlockSpec((hb, C, LANES), lambda b, h, i: (2 * nhb + h, n_lat // C + b, 0)),
                  pl.BlockSpec(memory_space=pl.ANY)],
        out_specs=pl.BlockSpec((C, hb * d), lambda b, h, i: (n_lat // C + b, h)),
        out_shape=jax.ShapeDtypeStruct((R, H * d), BF16),
        input_output_aliases={3: 0},
        compiler_params=_cp(("arbitrary", "arbitrary", "arbitrary")),
        name="na_attn_ctx",
    )(qkv, qkv, qkv, out)


def _sw_kernel(sink_ref, q_ref, kc_ref, vc_ref, *rest, has_loc, tq, band, S, G):
    if has_loc:
        k_ref, v_ref, o_ref = rest
    else:
        _, o_ref = rest
    hd = SW_HEAD_DIM
    per_blk = LANES // hd
    g2n = G // per_blk
    C = kc_ref.shape[1]
    c, i = pl.program_id(1), pl.program_id(2)
    lane = lax.broadcasted_iota(I32, (1, LANES), 1)

    def block_diag(x, par):
        sw = jnp.concatenate([x[:, hd:], x[:, :hd]], axis=1)
        zero = jnp.zeros_like(x)
        top = jnp.where(lane < hd, x if par == 0 else sw, zero)
        bot = jnp.where(lane >= hd, sw if par == 0 else x, zero)
        return jnp.concatenate([top, bot], axis=0)

    if has_loc:
        bstart = pl.multiple_of(jnp.clip(i * tq - SW_WINDOW, 0, S - band), LANES)
        kb = k_ref[0, pl.ds(bstart, band), :]
        vb = v_ref[0, pl.ds(bstart, band), :]
        qpos = i * tq + lax.broadcasted_iota(I32, (tq, 1), 0)
        kpos = bstart + lax.broadcasted_iota(I32, (1, band), 1)
        valid = jnp.abs(kpos - qpos) <= SW_WINDOW

    for par in range(2):
        kc_bd, vc_bd = block_diag(kc_ref[0], par), block_diag(vc_ref[0], par)
        if has_loc:
            kl_bd, vl_bd = block_diag(kb, par), block_diag(vb, par)
        for g2 in range(g2n):
            blk = par * g2n + g2
            q = q_ref[blk]
            sc2 = _dot_t(q, kc_bd)
            if has_loc:
                sl2 = _dot_t(q, kl_bd)
            pcs, pls, rls = [], [], []
            for hh in range(per_blk):
                sink = sink_ref[(2 * c + par) * G + g2 * per_blk + hh] * LOG2E
                sc = sc2[:, hh * C:(hh + 1) * C]
                mx = jnp.maximum(jnp.max(sc, axis=-1, keepdims=True), sink)
                if has_loc:
                    sl = jnp.where(valid, sl2[:, hh * band:(hh + 1) * band], NEG_BIG)
                    mx = jnp.maximum(mx, jnp.max(sl, axis=-1, keepdims=True))
                pc = jnp.exp2(sc - mx)
                l = jnp.sum(pc, axis=-1, keepdims=True) + jnp.exp2(sink - mx)
                pcs.append(pc)
                if has_loc:
                    pl_ = jnp.exp2(sl - mx)
                    l = l + jnp.sum(pl_, axis=-1, keepdims=True)
                    pls.append(pl_)
                rls.append(1.0 / l)
            o = _dot(jnp.concatenate(pcs, axis=1).astype(BF16), vc_bd)
            if has_loc:
                o = o + _dot(jnp.concatenate(pls, axis=1).astype(BF16), vl_bd)
            o = o * jnp.where(lane < hd, rls[0], rls[1])
            o_ref[:, blk * LANES:(blk + 1) * LANES] = o.astype(BF16)


def _sw_attention(qkv, sink, n_batch, S, C, H, tq):
    R = qkv.shape[1]
    n_lat = n_batch * S
    hd, KV = SW_HEAD_DIM, SW_KV_HEADS
    G = H // KV
    nqb, nkb = H * hd // LANES, KV * hd // LANES
    qpk = 2 * G * hd // LANES
    assert KV % 2 == 0 and LANES // hd == 2 and G % 2 == 0
    tq = _tile(S, tq)
    band = tq + 2 * SW_WINDOW
    assert band <= S and SW_WINDOW % LANES == 0
    smem = pl.BlockSpec(memory_space=pltpu.SMEM)
    out = pl.pallas_call(
        functools.partial(_sw_kernel, has_loc=True, tq=tq, band=band, S=S, G=G),
        grid=(n_batch, KV // 2, S // tq),
        in_specs=[smem,
                  pl.BlockSpec((qpk, tq, LANES), lambda b, c, i: (c, b * (S // tq) + i, 0)),
                  pl.BlockSpec((1, C, LANES), lambda b, c, i: (nqb + c, n_lat // C + b, 0)),
                  pl.BlockSpec((1, C, LANES), lambda b, c, i: (nqb + nkb + c, n_lat // C + b, 0)),
                  pl.BlockSpec((1, S, LANES), lambda b, c, i: (nqb + c, b, 0)),
                  pl.BlockSpec((1, S, LANES), lambda b, c, i: (nqb + nkb + c, b, 0))],
        out_specs=pl.BlockSpec((tq, qpk * LANES), lambda b, c, i: (b * (S // tq) + i, c)),
        out_shape=jax.ShapeDtypeStruct((R, H * hd), BF16),
        compiler_params=_cp(("arbitrary", "arbitrary", "arbitrary")),
        name="sw_attn_lat",
    )(sink, qkv, qkv, qkv, qkv, qkv)
    return pl.pallas_call(
        functools.partial(_sw_kernel, has_loc=False, tq=C, band=0, S=S, G=G),
        grid=(n_batch, KV // 2, 1),
        in_specs=[smem,
                  pl.BlockSpec((qpk, C, LANES), lambda b, c, i: (c, n_lat // C + b, 0)),
                  pl.BlockSpec((1, C, LANES), lambda b, c, i: (nqb + c, n_lat // C + b, 0)),
                  pl.BlockSpec((1, C, LANES), lambda b, c, i: (nqb + nkb + c, n_lat // C + b, 0)),
                  pl.BlockSpec(memory_space=pl.ANY)],
        out_specs=pl.BlockSpec((C, qpk * LANES), lambda b, c, i: (n_lat // C + b, c)),
        out_shape=jax.ShapeDtypeStruct((R, H * hd), BF16),
        input_output_aliases={4: 0},
        compiler_params=_cp(("arbitrary", "arbitrary", "arbitrary")),
        name="sw_attn_ctx",
    )(sink, qkv, qkv, qkv, out)


def _moe_plan(eidx, tme):
    e_flat = eidx[:2].reshape(-1)
    onehot = (e_flat[:, None] == jnp.arange(N_EXPERTS, dtype=I32)[None, :]).astype(I32)
    csum = jnp.cumsum(onehot, axis=0)
    rank = jnp.sum(csum * onehot, axis=1) - 1
    cnt = csum[-1]
    pcnt = (cnt + tme - 1) // tme * tme
    ends = jnp.cumsum(pcnt)
    offs = ends - pcnt
    dest = (offs[e_flat] + rank).astype(I32)
    pad_row = jnp.where(cnt % tme != 0, ends - tme, -1).astype(I32)
    return dest, (pcnt // tme).astype(I32), (offs // tme).astype(I32), pad_row


def _work_list(tiles_e, tile_base, n_chunks, n_items_max):
    items_e = tiles_e * n_chunks
    ends = jnp.cumsum(items_e)
    nw = ends[-1]
    w = jnp.minimum(jnp.arange(n_items_max, dtype=I32), nw - 1)
    e = jnp.sum((w[:, None] >= ends[None, :]).astype(I32), axis=1)
    local = w - (ends[e] - items_e[e])
    te = jnp.maximum(tiles_e[e], 1)
    chunk, r = local // te, local % te
    return (e.astype(I32), chunk.astype(I32), (tile_base[e] + r).astype(I32),
            (r == 0).astype(I32), nw.astype(I32).reshape(1))


def _token_copy(src, src_tok, dst, dst_tok, k, sem):
    def first_row(tok):
        return tok * k if isinstance(tok, int) else pl.multiple_of(tok * k, k)

    return pltpu.make_async_copy(src.at[pl.ds(first_row(src_tok), k)], dst.at[pl.ds(first_row(dst_tok), k)], sem)


def _scatter_rows_kernel(dest_ref, pad_ref, hp_ref, o_hbm, zbuf, sem, zsem, *, ts, n_rows, tme, k):
    def zero_copy(e):
        row = pl.multiple_of(pad_ref[e] * k, 8)
        return pltpu.make_async_copy(zbuf, o_hbm.at[pl.ds(row, tme * k)], zsem)

    @pl.when(pl.program_id(0) == 0)
    def _():
        zbuf[...] = jnp.zeros_like(zbuf)
        for e in range(N_EXPERTS):
            pl.when(pad_ref[e] >= 0)(lambda e=e: zero_copy(e).start())
        for e in range(N_EXPERTS):
            pl.when(pad_ref[e] >= 0)(lambda e=e: zero_copy(e).wait())

    base = pl.program_id(0) * ts

    def issue(r, carry):
        for slot in range(2):
            _token_copy(hp_ref, r, o_hbm, dest_ref[slot * n_rows + base + r], k, sem).start(priority=slot)
        return carry

    def drain(r, carry):
        for slot in range(2):
            _token_copy(hp_ref, r, o_hbm, 0, k, sem).wait()
        return carry

    lax.fori_loop(0, ts, issue, 0)
    lax.fori_loop(0, ts, drain, 0)


def _scatter_rows(hp, dest, pad_row, n_rows, tme, p_max):
    k = hp.shape[0] // n_rows
    ts = _tile(n_rows, 256, 8)
    return pl.pallas_call(
        functools.partial(_scatter_rows_kernel, ts=ts, n_rows=n_rows, tme=tme, k=k),
        grid_spec=pltpu.PrefetchScalarGridSpec(
            num_scalar_prefetch=2, grid=(n_rows // ts,),
            in_specs=[pl.BlockSpec((ts * k, LANES), lambda i, d, p: (i, 0))],
            out_specs=pl.BlockSpec(memory_space=pl.ANY),
            scratch_shapes=[pltpu.VMEM((tme * k, LANES), hp.dtype), pltpu.SemaphoreType.DMA(()),
                            pltpu.SemaphoreType.DMA(())]),
        out_shape=jax.ShapeDtypeStruct((p_max * k, LANES), hp.dtype),
        compiler_params=_cp(("arbitrary",)),
        name="moe_scatter",
    )(dest, pad_row, hp)


def _silu(g):
    return g / (1.0 + jnp.exp(-g))


def _moe_up_kernel(ie, inext, it, ifirst, nw, hp_ref, wg_hbm, wu_hbm, a_ref, wg32, wu32, wgb, wub, sem, *, layer):
    w = pl.program_id(0)

    def fetch(e):
        return (pltpu.make_async_copy(wg_hbm.at[layer, e], wg32, sem.at[0]),
                pltpu.make_async_copy(wu_hbm.at[layer, e], wu32, sem.at[1]))

    @pl.when(w == 0)
    def _():
        for copy in fetch(ie[0]):
            copy.start()

    @pl.when(w < nw[0])
    def _():
        @pl.when(ifirst[w] == 1)
        def _():
            for copy in fetch(ie[w]):
                copy.wait()
            wgb[...] = wg32[...].astype(BF16)
            wub[...] = wu32[...].astype(BF16)

            @pl.when(inext[w] >= 0)
            def _():
                for copy in fetch(inext[w]):
                    copy.start()

        hp = jnp.concatenate(_load_token_major(hp_ref, a_ref.shape[0]), axis=1)
        half = hp.shape[1]
        lo, hi = (v.astype(BF16) for v in _unpack_bf16_pair(hp))
        g = _dot(lo, wgb[:half]) + _dot(hi, wgb[half:])
        u = _dot(lo, wub[:half]) + _dot(hi, wub[half:])
        a_ref[...] = (_silu(g) * u).astype(BF16)


def _moe_down_kernel(ie, ic, it, ifirst, nw, a_ref, wd_ref, y_ref, wdb):
    w = pl.program_id(0)

    @pl.when(w < nw[0])
    def _():
        @pl.when(ifirst[w] == 1)
        def _():
            wdb[...] = wd_ref[...].astype(BF16)

        half = y_ref.shape[1]
        width = min(half, MXU_COLS)
        a = a_ref[...]
        for c0 in range(0, half, width):
            lo = _dot(a, wdb[:, c0:c0 + width])
            hi = _dot(a, wdb[:, half + c0:half + c0 + width])
            y_ref[:, c0:c0 + width] = _pack_bf16_pair(lo, hi)


def _pack_bf16_pair(lo, hi):
    lo_bits = pltpu.bitcast(lo.astype(BF16).astype(F32), U32)
    hi_bits = pltpu.bitcast(hi.astype(BF16).astype(F32), U32)
    return (lo_bits >> 16) | (hi_bits & jnp.uint32(0xFFFF0000))


def _unpack_bf16_pair(p):
    return pltpu.bitcast(p << 16, F32), pltpu.bitcast(p & jnp.uint32(0xFFFF0000), F32)


def _row_copy(src_hbm, src_row, dst, dst_row, sem):
    return pltpu.make_async_copy(src_hbm.at[pl.ds(src_row, 1)], dst.at[pl.ds(dst_row, 1)], sem)


def _combine_kernel(dest_ref, y_hbm, x_ref, mod_ref, gate_ref, *rest, tc, n_rows, mod_k, next_norm):
    if next_norm:
        g_next_ref, mod_next_ref, o_ref, h_ref, buf, sem = rest
    else:
        o_ref, buf, sem = rest
    i, n_steps = pl.program_id(0), pl.num_programs(0)

    def gather(step, start):
        tile_buf, tile_sem = buf.at[step % 2], sem.at[step % 2]

        def body(r, carry):
            for slot in range(2):
                src_row = dest_ref[slot * n_rows + step * tc + r] if start else 0
                copy = _row_copy(y_hbm, src_row, tile_buf.at[slot], r, tile_sem)
                copy.start(priority=slot) if start else copy.wait()
            return carry

        lax.fori_loop(0, tc, body, 0)

    pl.when(i == 0)(lambda: gather(i, True))
    pl.when(i + 1 < n_steps)(lambda: gather(i + 1, True))
    gather(i, False)
    cur = buf.at[i % 2]
    half = buf.shape[3]
    g0, g1 = gate_ref[:, 0:1], gate_ref[:, 1:2]
    width = min(half, 4 * LANES)
    for c0 in range(0, half, width):
        cols = slice(c0, c0 + width)
        y0_lo, y0_hi = _unpack_bf16_pair(cur[0, :, cols])
        y1_lo, y1_hi = _unpack_bf16_pair(cur[1, :, cols])
        for off, y in ((0, g0 * y0_lo + g1 * y1_lo), (half, g0 * y0_hi + g1 * y1_hi)):
            out_cols = slice(off + c0, off + c0 + width)
            o_ref[:, out_cols] = x_ref[:, out_cols] + mod_ref[0, mod_k:mod_k + 1, out_cols] * y
    if next_norm:
        shift, scale = next_norm
        h = _norm_rows(o_ref[...], g_next_ref[...])
        h_ref[...] = (h * (1.0 + mod_next_ref[0, scale:scale + 1, :]) + mod_next_ref[0, shift:shift + 1, :]).astype(BF16)


def _moe(X, hp, eidx, gw, w_gate, w_up, w_down, layer, mods, mod_k, midx, next_norm=None):
    R, D = X.shape
    d_ff = w_gate.shape[3]
    tme = MOE_ROW_TILE
    p_max = -(-(2 * R + N_EXPERTS * (tme - 1)) // tme) * tme
    t_max = p_max // tme
    dest, tiles_e, tile_base, pad_row = _moe_plan(eidx, tme)
    hs = _scatter_rows(hp, dest, pad_row, R, tme, p_max)
    kp = D // 2 // LANES

    items = _work_list(tiles_e, tile_base, 1, t_max)
    e_ids = jnp.arange(N_EXPERTS, dtype=I32)
    routed_at_or_after = lax.cummin(jnp.where(tiles_e > 0, e_ids, N_EXPERTS)[::-1])[::-1]
    next_routed = jnp.concatenate([routed_at_or_after[1:], jnp.full((1,), N_EXPERTS, I32)])
    next_routed = jnp.where(next_routed < N_EXPERTS, next_routed, -1).astype(I32)
    ie, _, it_, ifirst, nw = items
    a = pl.pallas_call(
        functools.partial(_moe_up_kernel, layer=layer),
        grid_spec=pltpu.PrefetchScalarGridSpec(
            num_scalar_prefetch=5, grid=(t_max,),
            in_specs=[pl.BlockSpec((tme * kp, LANES), lambda w, ie, nx, it, fi, nw: (it[w], 0)),
                      pl.BlockSpec(memory_space=pl.ANY), pl.BlockSpec(memory_space=pl.ANY)],
            out_specs=pl.BlockSpec((tme, d_ff), lambda w, ie, nx, it, fi, nw: (it[w], 0)),
            scratch_shapes=[pltpu.VMEM((D, d_ff), F32), pltpu.VMEM((D, d_ff), F32),
                            pltpu.VMEM((D, d_ff), BF16), pltpu.VMEM((D, d_ff), BF16),
                            pltpu.SemaphoreType.DMA((2,))]),
        out_shape=jax.ShapeDtypeStruct((p_max, d_ff), BF16),
        compiler_params=_cp(("arbitrary",)),
        name="moe_up",
    )(ie, next_routed[ie], it_, ifirst, nw, hs, w_gate, w_up)

    y = pl.pallas_call(
        _moe_down_kernel,
        grid_spec=pltpu.PrefetchScalarGridSpec(
            num_scalar_prefetch=5, grid=(t_max,),
            in_specs=[pl.BlockSpec((tme, d_ff), lambda w, ie, ic, it, fi, nw: (it[w], 0)),
                      pl.BlockSpec((None, None, d_ff, D), lambda w, ie, ic, it, fi, nw: (layer, ie[w], 0, 0))],
            out_specs=pl.BlockSpec((tme, D // 2), lambda w, ie, ic, it, fi, nw: (it[w], 0)),
            scratch_shapes=[pltpu.VMEM((d_ff, D), BF16)]),
        out_shape=jax.ShapeDtypeStruct((p_max, D // 2), U32),
        compiler_params=_cp(("arbitrary",)),
        name="moe_down",
    )(*items, a, w_down)

    tc = _tile(R, 128, 8)
    mod_of_tile = midx(tc)
    row_spec = pl.BlockSpec((tc, D), lambda i, d: (i, 0))
    mod_spec = pl.BlockSpec((1, N_MOD, D), lambda i, d: (mod_of_tile(i), 0, 0))
    in_specs = [pl.BlockSpec(memory_space=pl.ANY), row_spec, mod_spec, pl.BlockSpec((tc, 2), lambda i, d: (i, 0))]
    operands = [dest, y, X, mods, gw[:2].T]
    out_specs, out_shape = row_spec, jax.ShapeDtypeStruct((R, D), F32)
    if next_norm is not None:
        g_next, mods_next, shift, scale = next_norm
        in_specs += [pl.BlockSpec((1, D), lambda i, d: (0, 0)), mod_spec]
        operands += [g_next.reshape(1, D), mods_next]
        out_specs, out_shape = [row_spec, row_spec], [out_shape, jax.ShapeDtypeStruct((R, D), BF16)]
    return pl.pallas_call(
        functools.partial(_combine_kernel, tc=tc, n_rows=R, mod_k=mod_k,
                          next_norm=None if next_norm is None else (shift, scale)),
        grid_spec=pltpu.PrefetchScalarGridSpec(
            num_scalar_prefetch=1, grid=(R // tc,),
            in_specs=in_specs,
            out_specs=out_specs,
            scratch_shapes=[pltpu.VMEM((2, 2, tc, D // 2), U32), pltpu.SemaphoreType.DMA((2,))]),
        out_shape=out_shape,
        input_output_aliases={2: 0},
        compiler_params=_cp(("arbitrary",)),
        name="moe_combine",
    )(*operands)


def kernel(x, c, ctx, c_ctx, cond_down, mod_w, mod_b, norm_g, final_norm_g, da_w_qkv, da_w_o, da_lambda, da_subln_g, na_w_qkv, na_w_o, na_rpb, sw_w_qkv, sw_w_o, sw_sink, router_w, router_bias, moe_w_gate, moe_w_up, moe_w_down):
    B, S, D = x.shape
    C = ctx.shape[1]
    depth = mod_w.shape[0]
    n_lat, n_ctx = B * S, B * C
    R = n_lat + n_ctx

    def midx(tm):
        assert S % tm == 0 and n_ctx % tm == 0
        return _mod_index_fn(n_lat // tm, S // tm, B)

    X = jnp.concatenate([x.reshape(n_lat, D), ctx.reshape(n_ctx, D)], axis=0)

    cond_rows = 16
    cin = jnp.concatenate([c, c_ctx[None, :], jnp.zeros((cond_rows - B - 1, D), F32)], axis=0)
    cin = jax.nn.silu(cin).astype(BF16)
    cond = _mm_bias(cin, cond_down[None], 0, jnp.zeros((1, cond_down.shape[1]), F32), BF16)

    rw_pad = jnp.zeros((D, LANES), F32).at[:, :N_EXPERTS].set(router_w).astype(BF16)
    rb = router_bias.astype(F32).reshape(N_EXPERTS, 1)

    tm_row = _tile(math.gcd(S, n_ctx), 256, 8)
    tm_mm = _tile(math.gcd(S, n_ctx), 512, 8)
    da_h = da_w_qkv.shape[2] // (6 * DA_HEAD_DIM)
    na_h = na_w_qkv.shape[2] // (3 * NA_HEAD_DIM)
    sw_h = sw_sink.shape[1]
    tabs = {}

    all_mods = [_mm_bias(cond, mod_w, i, mod_b[i][None, :], F32)[:B + 1].reshape(B + 1, N_MOD, D)
                for i in range(depth)]
    h = _norm_mod(X, norm_g[0, 0], all_mods[0], 0, 1, tm_row, midx(tm_row))
    for i in range(depth):
        mods = all_mods[i]
        kind, j = i % N_MIXERS, i // N_MIXERS
        if kind == 0:
            d = DA_HEAD_DIM
            qd = da_h * 2 * d
            tn = _tile(qd, 1024)
            if "da" not in tabs:
                tabs["da"] = _rope_tables(S, B, n_ctx, d, d ** -0.5 * LOG2E)
            qkv = _mm_qkv(h, da_w_qkv, j, tabs["da"], tm_mm, tn, qd // tn, 2 * qd // tn, 1.0, d // 4)
            lam_init = 0.8 - 0.6 * math.exp(-0.3 * i)
            lp = da_lambda[j].astype(F32)
            lam = (jnp.exp(jnp.sum(lp[0] * lp[1])) - jnp.exp(jnp.sum(lp[2] * lp[3])) + lam_init).reshape(1)
            o = _da_attention(qkv, lam, da_subln_g[j], 1.0 - lam_init, B, S, C, da_h, 512)
            w_o = da_w_o
        elif kind == 1:
            d = NA_HEAD_DIM
            tn = _tile(na_h * d, 1024)
            qkv = _mm_qkv(h, na_w_qkv, j, None, tm_mm, tn, na_h * d // tn, 0, d ** -0.5 * LOG2E, 0)
            o = _na_attention(qkv, _na_bias(na_rpb[j], S // GRID_W), B, S, C, na_h)
            w_o = na_w_o
        else:
            d = SW_HEAD_DIM
            nq, nkv = sw_h * d, SW_KV_HEADS * d
            tn = _tile(math.gcd(nq, nkv), 512)
            if "sw" not in tabs:
                tabs["sw"] = _rope_tables(S, B, n_ctx, d, d ** -0.5 * LOG2E)
            qkv = _mm_qkv(h, sw_w_qkv, j, tabs["sw"], tm_mm, tn, nq // tn, (nq + nkv) // tn, 1.0, d // 4)
            o = _sw_attention(qkv, sw_sink[j].astype(F32), B, S, C, sw_h, 256)
            w_o = sw_w_o
        X = _mm_res(o, w_o, j, X, mods, 2, tm_mm, midx(tm_mm))
        hp, eidx, gw = _norm_router(X, norm_g[i, 1], mods, 3, 4, rw_pad, rb, tm_row, midx(tm_row))
        if i + 1 < depth:
            X, h = _moe(X, hp, eidx, gw, moe_w_gate, moe_w_up, moe_w_down, i, mods, 5, midx,
                        next_norm=(norm_g[i + 1, 0], all_mods[i + 1], 0, 1))
        else:
            X = _moe(X, hp, eidx, gw, moe_w_gate, moe_w_up, moe_w_down, i, mods, 5, midx)

    tm_fin = _tile(S, 256, 8)
    return _final_norm(X, final_norm_g, n_lat, tm_fin).reshape(B, S, D)
```
